```python
import jax, jax.numpy as jnp
from jax import lax
import numpy as np

D_MODEL = 1024
BATCH = 2
SEQ = 16384
DEPTH = 1

N_MEM = 256
EPS = 1e-6
QBLK = 128
MLA_HEADS = 4
MLA_NOPE = 128
MLA_ROPE = 64
MLA_V = 128
MLA_Q_RANK = 256
MLA_KV_RANK = 128
ROPE_THETA = 10000.0
SWA_HEADS = 8
SWA_KV_HEADS = 2
SWA_HD = 64
WINDOW = 128
X_HEADS = 4
X_HD = 128
PEER_HEADS = 8
N_KEYS = 128
N_EXPERTS = N_KEYS * N_KEYS
PEER_QDIM = 256
PEER_HALF = PEER_QDIM // 2
PEER_TOPK = 16
PEER_CHUNK = 128

MLA_WIDTH = MLA_HEADS * MLA_V
SWA_WIDTH = SWA_HEADS * SWA_HD
MIX_WIDTH = MLA_WIDTH + SWA_WIDTH
SWA_KV_WIDTH = SWA_KV_HEADS * SWA_HD
IN_COLS = MLA_Q_RANK + MLA_KV_RANK + MLA_ROPE + SWA_WIDTH + 2 * SWA_KV_WIDTH
SPLITS = (MLA_Q_RANK,
          MLA_Q_RANK + MLA_KV_RANK,
          MLA_Q_RANK + MLA_KV_RANK + MLA_ROPE,
          MLA_Q_RANK + MLA_KV_RANK + MLA_ROPE + SWA_WIDTH,
          MLA_Q_RANK + MLA_KV_RANK + MLA_ROPE + SWA_WIDTH + SWA_KV_WIDTH)

kernel_name = 'hymba_mla_swa_peer_block'


def rmsnorm(x, g):
    xf = x.astype(jnp.float32)
    y = xf * lax.rsqrt(jnp.mean(xf * xf, axis=-1, keepdims=True) + EPS)
    return (y * g.astype(jnp.float32)).astype(x.dtype)


def rope(x, positions):
    half = x.shape[-1] // 2
    inv = ROPE_THETA ** (-jnp.arange(half, dtype=jnp.float32) / half)
    ang = positions.astype(jnp.float32)[..., None] * inv
    ang = ang.reshape(ang.shape[:2] + (1,) * (x.ndim - 3) + (half,))
    cos, sin = jnp.cos(ang), jnp.sin(ang)
    x1 = x[..., :half].astype(jnp.float32)
    x2 = x[..., half:].astype(jnp.float32)
    out = jnp.concatenate([x1 * cos - x2 * sin, x2 * cos + x1 * sin], axis=-1)
    return out.astype(x.dtype)


def alibi_slopes(n_heads):
    return 2.0 ** (-(8.0 / n_heads) * jnp.arange(1, n_heads + 1, dtype=jnp.float32))


def mla_attention(c_q, c_kv, k_rope, positions, q_a_norm, w_q_b, kv_a_norm, w_kv_b):
    B, S, _ = c_q.shape
    q = (rmsnorm(c_q, q_a_norm) @ w_q_b).reshape(B, S, MLA_HEADS, MLA_NOPE + MLA_ROPE)
    q_nope, q_rope = q[..., :MLA_NOPE], rope(q[..., MLA_NOPE:], positions)
    c = rmsnorm(c_kv, kv_a_norm)
    k_r = rope(k_rope, positions)
    w = w_kv_b.reshape(MLA_KV_RANK, MLA_HEADS, MLA_NOPE + MLA_V)
    w_uk, w_uv = w[..., :MLA_NOPE], w[..., MLA_NOPE:]
    q_lat = jnp.einsum('bshd,rhd->bshr', q_nope, w_uk)
    q_cat = jnp.concatenate([q_lat, q_rope], axis=-1)
    k_cat = jnp.concatenate([c, k_r], axis=-1)
    scale = (MLA_NOPE + MLA_ROPE) ** -0.5
    nblk = S // QBLK
    qb = q_cat.reshape(B, nblk, QBLK, MLA_HEADS, -1).transpose(1, 0, 2, 3, 4)
    k_idx = jnp.arange(S)

    def block(args):
        i, q_i = args
        s = jnp.einsum('bqhd,bkd->bhqk', q_i, k_cat).astype(jnp.float32) * scale
        q_idx = i * QBLK + jnp.arange(QBLK)
        s = jnp.where(k_idx[None, :] <= q_idx[:, None], s, -jnp.inf)
        p = jax.nn.softmax(s, axis=-1).astype(c.dtype)
        return jnp.einsum('bhqk,bkr->bqhr', p, c)

    o_lat = lax.map(block, (jnp.arange(nblk), qb))
    o_lat = o_lat.transpose(1, 0, 2, 3, 4).reshape(B, S, MLA_HEADS, MLA_KV_RANK)
    return jnp.einsum('bshr,rhd->bshd', o_lat, w_uv).reshape(B, S, MLA_WIDTH)


def swa_attention(q, k, v, sinks):
    B, S, _ = q.shape
    nblk = S // QBLK
    G = SWA_HEADS // SWA_KV_HEADS
    q = q.reshape(B, nblk, QBLK, SWA_KV_HEADS, G, SWA_HD)
    k = k.reshape(B, S, SWA_KV_HEADS, SWA_HD)
    v = v.reshape(B, S, SWA_KV_HEADS, SWA_HD)
    pad = jnp.zeros((B, QBLK, SWA_KV_HEADS, SWA_HD), k.dtype)
    kp = jnp.concatenate([pad, k], axis=1).reshape(B, nblk + 1, QBLK, SWA_KV_HEADS, SWA_HD)
    vp = jnp.concatenate([pad, v], axis=1).reshape(B, nblk + 1, QBLK, SWA_KV_HEADS, SWA_HD)
    k_band = jnp.concatenate([kp[:, :-1], kp[:, 1:]], axis=2)
    v_band = jnp.concatenate([vp[:, :-1], vp[:, 1:]], axis=2)
    s = jnp.einsum('bnqkgd,bnjkd->bnkgqj', q, k_band).astype(jnp.float32) * (SWA_HD ** -0.5)
    a = jnp.arange(QBLK)[:, None]
    j = jnp.arange(2 * QBLK)[None, :]
    dist = a + QBLK - j
    key_abs = jnp.arange(nblk)[:, None] * QBLK - QBLK + jnp.arange(2 * QBLK)[None, :]
    valid = ((dist >= 0) & (dist < WINDOW))[None] & (key_abs >= 0)[:, None, :]
    slopes = alibi_slopes(SWA_HEADS).reshape(SWA_KV_HEADS, G)
    s = s - slopes[:, :, None, None] * dist.astype(jnp.float32)
    s = jnp.where(valid[None, :, None, None], s, -jnp.inf)
    sink = sinks.astype(jnp.float32).reshape(SWA_KV_HEADS, G)[:, :, None, None]
    m = jnp.maximum(jnp.max(s, axis=-1, keepdims=True), sink)
    e = jnp.exp(s - m)
    p = e / (jnp.sum(e, axis=-1, keepdims=True) + jnp.exp(sink - m))
    o = jnp.einsum('bnkgqj,bnjkd->bnqkgd', p.astype(v.dtype), v_band)
    return o.reshape(B, S, SWA_WIDTH)


def cross_attention(h, mem, norm_mem, w_cq, w_ck, w_cv, w_co):
    B, S, _ = h.shape
    mn = rmsnorm(mem, norm_mem)
    q = (h @ w_cq).reshape(B, S, X_HEADS, X_HD)
    k = (mn @ w_ck).reshape(B, -1, X_HEADS, X_HD)
    v = (mn @ w_cv).reshape(B, -1, X_HEADS, X_HD)
    s = jnp.einsum('bshd,bmhd->bhsm', q, k).astype(jnp.float32) * (X_HD ** -0.5)
    p = jax.nn.softmax(s, axis=-1).astype(v.dtype)
    o = jnp.einsum('bhsm,bmhd->bshd', p, v).reshape(B, S, X_HEADS * X_HD)
    return o @ w_co


def peer_ffn(h, w_q, keys, u, v):
    B, S, D = h.shape
    q = (h @ w_q).reshape(B, S, PEER_HEADS, 2, PEER_HALF)
    sc = jnp.einsum('bshcd,hcnd->bshcn', q, keys).astype(jnp.float32)
    top_s, top_i = lax.top_k(sc, PEER_TOPK)
    cand_s = (top_s[..., 0, :, None] + top_s[..., 1, None, :]).reshape(B, S, PEER_HEADS, PEER_TOPK * PEER_TOPK)
    cand_i = (top_i[..., 0, :, None] * N_KEYS + top_i[..., 1, None, :]).reshape(B, S, PEER_HEADS, PEER_TOPK * PEER_TOPK)
    best_s, best_pos = lax.top_k(cand_s, PEER_TOPK)
    idx = jnp.take_along_axis(cand_i, best_pos, axis=-1)
    g = jax.nn.softmax(best_s, axis=-1).astype(h.dtype)
    nch = S // PEER_CHUNK
    h_c = h.reshape(B, nch, PEER_CHUNK, D).swapaxes(0, 1)
    idx_c = idx.reshape(B, nch, PEER_CHUNK, PEER_HEADS, PEER_TOPK).swapaxes(0, 1)
    g_c = g.reshape(B, nch, PEER_CHUNK, PEER_HEADS, PEER_TOPK).swapaxes(0, 1)

    def chunk(args):
        hc, ic, gc = args
        act = jax.nn.gelu(jnp.einsum('bcd,bchkd->bchk', hc, u[ic]), approximate=False)
        return jnp.einsum('bchk,bchkd->bcd', gc * act, v[ic])

    y = lax.map(chunk, (h_c, idx_c, g_c))
    return y.swapaxes(0, 1).reshape(B, S, D)


def setup_inputs(seed: int = 0) -> dict:
    key = jax.random.key(seed)
    ks = jax.random.split(key, 26)
    L, D = DEPTH, D_MODEL

    def nrm(k, shape, scale):
        return jax.random.normal(k, shape, jnp.float32) * scale

    def gain(k, shape):
        return 1.0 + 0.05 * jax.random.normal(k, shape, jnp.float32)

    start = jax.random.randint(ks[2], (BATCH, 1), 0, 4096, dtype=jnp.int32)
    positions = start + jnp.arange(SEQ, dtype=jnp.int32)[None, :]
    return {
        'x': nrm(ks[0], (BATCH, SEQ, D), 1.0),
        'mem': nrm(ks[1], (BATCH, N_MEM, D), 1.0),
        'positions': positions,
        'norm_mix': gain(ks[3], (L, D)),
        'w_in': nrm(ks[4], (L, D, IN_COLS), D ** -0.5),
        'q_a_norm': gain(ks[5], (L, MLA_Q_RANK)),
        'w_q_b': nrm(ks[6], (L, MLA_Q_RANK, MLA_HEADS * (MLA_NOPE + MLA_ROPE)), MLA_Q_RANK ** -0.5),
        'kv_a_norm': gain(ks[7], (L, MLA_KV_RANK)),
        'w_kv_b': nrm(ks[8], (L, MLA_KV_RANK, MLA_HEADS * (MLA_NOPE + MLA_V)), MLA_KV_RANK ** -0.5),
        'swa_sinks': nrm(ks[9], (L, SWA_HEADS), 1.0),
        'out_norm_mla': gain(ks[10], (L, MLA_WIDTH)),
        'out_norm_swa': gain(ks[11], (L, SWA_WIDTH)),
        'w_out': nrm(ks[12], (L, MIX_WIDTH, D), MIX_WIDTH ** -0.5),
        'norm_cross': gain(ks[13], (L, D)),
        'norm_mem': gain(ks[14], (L, D)),
        'w_cq': nrm(ks[15], (L, D, X_HEADS * X_HD), D ** -0.5),
        'w_ck': nrm(ks[16], (L, D, X_HEADS * X_HD), D ** -0.5),
        'w_cv': nrm(ks[17], (L, D, X_HEADS * X_HD), D ** -0.5),
        'w_co': nrm(ks[18], (L, X_HEADS * X_HD, D), (X_HEADS * X_HD) ** -0.5),
        'norm_ffn': gain(ks[19], (L, D)),
        'peer_w_q': nrm(ks[20], (L, D, PEER_HEADS * PEER_QDIM), D ** -0.5),
        'peer_keys': nrm(ks[21], (L, PEER_HEADS, 2, N_KEYS, PEER_HALF), PEER_HALF ** -0.5),
        'peer_u': nrm(ks[22], (L, N_EXPERTS, D), D ** -0.5),
        'peer_v': nrm(ks[23], (L, N_EXPERTS, D), 0.5),
        'norm_final': gain(ks[24], (D,)),
    }


def reference(x, mem, positions, norm_mix, w_in, q_a_norm, w_q_b, kv_a_norm, w_kv_b,
              swa_sinks, out_norm_mla, out_norm_swa, w_out, norm_cross, norm_mem,
              w_cq, w_ck, w_cv, w_co, norm_ffn, peer_w_q, peer_keys, peer_u, peer_v,
              norm_final):
    for l in range(DEPTH):
        h = rmsnorm(x, norm_mix[l])
        c_q, c_kv, k_rope, q_s, k_s, v_s = jnp.split(h @ w_in[l], list(SPLITS), axis=-1)
        o_a = mla_attention(c_q, c_kv, k_rope, positions, q_a_norm[l], w_q_b[l], kv_a_norm[l], w_kv_b[l])
        o_b = swa_attention(q_s, k_s, v_s, swa_sinks[l])
        mix = jnp.concatenate([rmsnorm(o_a, out_norm_mla[l]), rmsnorm(o_b, out_norm_swa[l])], axis=-1)
        x = x + mix @ w_out[l]
        x = x + cross_attention(rmsnorm(x, norm_cross[l]), mem, norm_mem[l], w_cq[l], w_ck[l], w_cv[l], w_co[l])
        x = x + peer_ffn(rmsnorm(x, norm_ffn[l]), peer_w_q[l], peer_keys[l], peer_u[l], peer_v[l])
    return rmsnorm(x, norm_final)
```

```python
import functools
import math

import jax
import jax.numpy as jnp
from jax import lax
from jax.experimental import pallas as pl
from jax.experimental.pallas import tpu as pltpu

F32 = jnp.float32
BF16 = jnp.bfloat16
NEG_INF = float("-inf")

EPS = 1e-6
ROPE_THETA = 10000.0
MLA_HEADS = 4
MLA_NOPE = 128
MLA_ROPE = 64
MLA_V = 128
MLA_Q_RANK = 256
MLA_KV_RANK = 128
MLA_QK = MLA_KV_RANK + MLA_ROPE
SWA_HEADS = 8
SWA_KV_HEADS = 2
SWA_HD = 64
SWA_BLK = 128
X_HEADS = 4
X_HD = 128
PEER_HEADS = 8
N_KEYS = 128
PEER_HALF = 128
PEER_TOPK = 16

LANES = 128
VMEM_LIMIT = 48 * 1024 * 1024


def _rms(x, g):
    return x * lax.rsqrt(jnp.mean(x * x, axis=-1, keepdims=True) + EPS) * g


def _dot(a, b):
    return jnp.dot(a, b, preferred_element_type=F32)


def _dot_nt(a, b):
    return lax.dot_general(a, b, (((1,), (1,)), ((), ())), preferred_element_type=F32)


def _const_spec(shape):
    zeros = (0,) * len(shape)
    return pl.BlockSpec(shape, lambda *_: zeros)


def _params(*sem):
    return pltpu.CompilerParams(dimension_semantics=sem, vmem_limit_bytes=VMEM_LIMIT)


def _memkv_kernel(mem_ref, g_ref, w_ref, k_ref, v_ref):
    mn = _rms(mem_ref[0], g_ref[...]).astype(BF16)
    kv = _dot(mn, w_ref[...])
    width = k_ref.shape[-1]
    k_ref[0] = kv[:, :width].astype(BF16)
    v_ref[0] = kv[:, width:].astype(BF16)


def _mem_kv(mem, norm_mem, w_ckv):
    B, M, D = mem.shape
    width = w_ckv.shape[1] // 2
    return pl.pallas_call(
        _memkv_kernel,
        grid=(B,),
        in_specs=[pl.BlockSpec((1, M, D), lambda b: (b, 0, 0)),
                  _const_spec((1, D)), _const_spec(w_ckv.shape)],
        out_specs=[pl.BlockSpec((1, M, width), lambda b: (b, 0, 0))] * 2,
        out_shape=[jax.ShapeDtypeStruct((B, M, width), BF16)] * 2,
        compiler_params=_params("arbitrary"),
        name="mem_kv",
    )(mem, norm_mem, w_ckv)


_C_CQ = 0
_C_CKV = _C_CQ + MLA_Q_RANK
_C_QS = _C_CKV + MLA_KV_RANK
_C_KS = _C_QS + SWA_HEADS * LANES
_C_VS = _C_KS + SWA_KV_HEADS * SWA_HD
_C_KR = _C_VS + SWA_KV_HEADS * SWA_HD
_C_END = _C_KR + 2 * MLA_ROPE


def _proj_kernel(x_ref, pos_ref, inv_ref, sgn_ref, nmix_ref, win_ref, qan_ref, wqb_ref, kvan_ref,
                 wuk_ref, qcat_ref, kcat_ref, qs_ref, ks_ref, vs_ref):
    h = _rms(x_ref[...], nmix_ref[...]).astype(BF16)
    proj = _dot(h, win_ref[...])
    qs_ref[...] = proj[:, _C_QS:_C_KS].astype(BF16)
    ks_ref[...] = proj[:, _C_KS:_C_VS].astype(BF16)
    vs_ref[...] = proj[:, _C_VS:_C_KR].astype(BF16)

    ang = pos_ref[...].astype(F32) * inv_ref[...]
    lane = lax.broadcasted_iota(jnp.int32, ang.shape, 1)
    cs = jnp.where(lane < MLA_ROPE, jnp.cos(ang), jnp.sin(ang) * sgn_ref[...])

    def rope_slot(slot):
        r = slot * cs
        return (r + pltpu.roll(r, MLA_ROPE, 1))[:, :MLA_ROPE]

    c = _rms(proj[:, _C_CKV:_C_QS], kvan_ref[...])
    k_r = rope_slot(proj[:, _C_KR:_C_END])
    kcat_ref[...] = jnp.concatenate([c, k_r], axis=1).astype(BF16)

    qn = _rms(proj[:, _C_CQ:_C_CKV], qan_ref[...]).astype(BF16)
    q2 = _dot(qn, wqb_ref[...])
    scale = float(MLA_NOPE + MLA_ROPE) ** -0.5
    rope_base = MLA_HEADS * MLA_NOPE
    for hd in range(MLA_HEADS):
        q_lat = _dot(q2[:, hd * MLA_NOPE:(hd + 1) * MLA_NOPE].astype(BF16), wuk_ref[hd])
        q_r = rope_slot(q2[:, rope_base + hd * LANES: rope_base + (hd + 1) * LANES])
        qcat_ref[hd] = (jnp.concatenate([q_lat, q_r], axis=1) * scale).astype(BF16)


def _proj(x2d, pos2d, inv_slot, sgn_slot, norm_mix, w_in_r, q_a_norm, w_qb_r, kv_a_norm, w_ukT, tile):
    T, D = x2d.shape
    qs_w = SWA_HEADS * LANES
    kv_w = SWA_KV_HEADS * SWA_HD
    row = lambda i: (i, 0)
    return pl.pallas_call(
        _proj_kernel,
        grid=(T // tile,),
        in_specs=[pl.BlockSpec((tile, D), row), pl.BlockSpec((tile, 1), row),
                  _const_spec(inv_slot.shape), _const_spec(sgn_slot.shape),
                  _const_spec(norm_mix.shape), _const_spec(w_in_r.shape),
                  _const_spec(q_a_norm.shape), _const_spec(w_qb_r.shape),
                  _const_spec(kv_a_norm.shape), _const_spec(w_ukT.shape)],
        out_specs=[pl.BlockSpec((MLA_HEADS, tile, MLA_QK), lambda i: (0, i, 0)),
                   pl.BlockSpec((tile, MLA_QK), row),
                   pl.BlockSpec((tile, qs_w), row),
                   pl.BlockSpec((tile, kv_w), row),
                   pl.BlockSpec((tile, kv_w), row)],
        out_shape=[jax.ShapeDtypeStruct((MLA_HEADS, T, MLA_QK), BF16),
                   jax.ShapeDtypeStruct((T, MLA_QK), BF16),
                   jax.ShapeDtypeStruct((T, qs_w), BF16),
                   jax.ShapeDtypeStruct((T, kv_w), BF16),
                   jax.ShapeDtypeStruct((T, kv_w), BF16)],
        compiler_params=_params("arbitrary"),
        name="proj",
    )(x2d, pos2d, inv_slot, sgn_slot, norm_mix, w_in_r, q_a_norm, w_qb_r, kv_a_norm, w_ukT)


def _mla_kernel(q_ref, k_ref, o_ref, m_ref, l_ref, acc_ref, *, tq, kb):
    i = pl.program_id(1)
    rows = MLA_HEADS * tq
    q = q_ref[...].reshape(rows, MLA_QK)
    m_ref[...] = jnp.full(m_ref.shape, NEG_INF, F32)
    l_ref[...] = jnp.zeros(l_ref.shape, F32)
    acc_ref[...] = jnp.zeros(acc_ref.shape, F32)

    def step(j, masked):
        k = k_ref[0, pl.ds(pl.multiple_of(j * kb, kb), kb), :]
        s = _dot_nt(q, k)
        if masked:
            q_idx = i * tq + (lax.broadcasted_iota(jnp.int32, s.shape, 0) & (tq - 1))
            k_idx = j * kb + lax.broadcasted_iota(jnp.int32, s.shape, 1)
            s = jnp.where(k_idx <= q_idx, s, NEG_INF)
        m_prev = m_ref[...]
        m_new = jnp.maximum(m_prev, jnp.max(s, axis=-1, keepdims=True))
        alpha = jnp.exp(m_prev - m_new)
        p = jnp.exp(s - m_new)
        l_ref[...] = alpha * l_ref[...] + jnp.sum(p, axis=-1, keepdims=True)
        acc_ref[...] = alpha * acc_ref[...] + _dot(p.astype(BF16), k[:, :MLA_KV_RANK])
        m_ref[...] = m_new

    n_full = (i * tq) // kb

    def body(j, carry):
        step(j, False)
        return carry

    lax.fori_loop(0, n_full, body, 0)
    step(n_full, True)
    o = acc_ref[...] / l_ref[...]
    for hd in range(MLA_HEADS):
        o_ref[:, hd * MLA_KV_RANK:(hd + 1) * MLA_KV_RANK] = o[hd * tq:(hd + 1) * tq].astype(BF16)


def _mla(qcat, kcat3, tq, kb):
    B, S, _ = kcat3.shape
    nq = S // tq
    rows = MLA_HEADS * tq
    return pl.pallas_call(
        functools.partial(_mla_kernel, tq=tq, kb=kb),
        grid=(B, nq),
        in_specs=[pl.BlockSpec((MLA_HEADS, tq, MLA_QK), lambda b, i: (0, b * nq + i, 0)),
                  pl.BlockSpec((1, S, MLA_QK), lambda b, i: (b, 0, 0))],
        out_specs=pl.BlockSpec((tq, MLA_HEADS * MLA_KV_RANK), lambda b, i: (b * nq + i, 0)),
        out_shape=jax.ShapeDtypeStruct((B * S, MLA_HEADS * MLA_KV_RANK), BF16),
        scratch_shapes=[pltpu.VMEM((rows, 1), F32), pltpu.VMEM((rows, 1), F32),
                        pltpu.VMEM((rows, MLA_KV_RANK), F32)],
        compiler_params=_params("arbitrary", "arbitrary"),
        name="mla",
    )(qcat, kcat3)


def _swa_kernel(sink_ref, q_ref, k_ref, kp_ref, v_ref, vp_ref, o_ref, kf_ref, vf_ref, *, ts):
    i = pl.program_id(1)
    kf_ref[0:SWA_BLK] = kp_ref[...]
    kf_ref[SWA_BLK:] = k_ref[...]
    vf_ref[0:SWA_BLK] = vp_ref[...]
    vf_ref[SWA_BLK:] = v_ref[...]
    a = lax.broadcasted_iota(jnp.int32, (SWA_BLK, SWA_BLK), 0)
    j = lax.broadcasted_iota(jnp.int32, (SWA_BLK, SWA_BLK), 1)
    dist_cur = (a - j).astype(F32)
    dist_prev = dist_cur + float(SWA_BLK)
    valid_cur = j <= a
    valid_prev = j > a
    low_half = j < SWA_HD
    scale = float(SWA_HD) ** -0.5
    group = SWA_HEADS // SWA_KV_HEADS
    for n in range(ts // SWA_BLK):
        r0 = n * SWA_BLK
        k_prev = kf_ref[r0:r0 + SWA_BLK]
        k_cur = kf_ref[r0 + SWA_BLK:r0 + 2 * SWA_BLK]
        v_prev = vf_ref[r0:r0 + SWA_BLK]
        v_cur = vf_ref[r0 + SWA_BLK:r0 + 2 * SWA_BLK]
        prev_pen = jnp.where(jnp.logical_and(i == 0, n == 0), NEG_INF, 0.0).astype(F32)
        outs = []
        for hh in range(SWA_HEADS):
            slope = 2.0 ** (-(8.0 / SWA_HEADS) * (hh + 1))
            qh = q_ref[r0:r0 + SWA_BLK, hh * LANES:(hh + 1) * LANES]
            sp = _dot_nt(qh, k_prev) * scale - slope * dist_prev
            sc = _dot_nt(qh, k_cur) * scale - slope * dist_cur
            sp = jnp.where(valid_prev, sp, NEG_INF) + prev_pen
            sc = jnp.where(valid_cur, sc, NEG_INF)
            sink = sink_ref[hh]
            m = jnp.maximum(jnp.maximum(jnp.max(sp, axis=-1, keepdims=True),
                                        jnp.max(sc, axis=-1, keepdims=True)), sink)
            ep = jnp.exp(sp - m)
            ec = jnp.exp(sc - m)
            den = (jnp.sum(ep, axis=-1, keepdims=True) + jnp.sum(ec, axis=-1, keepdims=True)
                   + jnp.exp(sink - m))
            o = (_dot(ep.astype(BF16), v_prev) + _dot(ec.astype(BF16), v_cur)) / den
            outs.append(o)
        for pair in range(SWA_HEADS // 2):
            kv = (2 * pair) // group
            oe, oo = outs[2 * pair], outs[2 * pair + 1]
            if kv == 0:
                both = jnp.where(low_half, oe, pltpu.roll(oo, SWA_HD, 1))
            else:
                both = jnp.where(low_half, pltpu.roll(oe, SWA_HD, 1), oo)
            o_ref[r0:r0 + SWA_BLK, pair * LANES:(pair + 1) * LANES] = both.astype(BF16)


def _swa(sinks, qs, ks, vs, B, S, ts):
    T = B * S
    nt = S // ts
    kv_w = SWA_KV_HEADS * SWA_HD
    blk_per_tile = ts // SWA_BLK
    cur = lambda b, i: (b * nt + i, 0)
    prev = lambda b, i: (jnp.maximum((b * nt + i) * blk_per_tile - 1, 0), 0)
    return pl.pallas_call(
        functools.partial(_swa_kernel, ts=ts),
        grid=(B, nt),
        in_specs=[pl.BlockSpec(memory_space=pltpu.SMEM),
                  pl.BlockSpec((ts, SWA_HEADS * LANES), cur),
                  pl.BlockSpec((ts, kv_w), cur), pl.BlockSpec((SWA_BLK, kv_w), prev),
                  pl.BlockSpec((ts, kv_w), cur), pl.BlockSpec((SWA_BLK, kv_w), prev)],
        out_specs=pl.BlockSpec((ts, SWA_HEADS * SWA_HD), cur),
        out_shape=jax.ShapeDtypeStruct((T, SWA_HEADS * SWA_HD), BF16),
        scratch_shapes=[pltpu.VMEM((ts + SWA_BLK, kv_w), BF16), pltpu.VMEM((ts + SWA_BLK, kv_w), BF16)],
        compiler_params=_params("arbitrary", "arbitrary"),
        name="swa",
    )(sinks, qs, ks, ks, vs, vs)


def _mid_kernel(x_ref, ol_ref, ob_ref, km_ref, vm_ref, wuv_ref, onm_ref, ons_ref, wout_ref,
                ncross_ref, wcq_ref, wco_ref, nffn_ref, wqT_ref, keys_ref,
                x2_ref, hf_ref, scT_ref):
    ol = ol_ref[...]
    o_a = jnp.concatenate(
        [_dot(ol[:, hd * MLA_KV_RANK:(hd + 1) * MLA_KV_RANK], wuv_ref[hd]) for hd in range(MLA_HEADS)],
        axis=1)
    mix = jnp.concatenate([_rms(o_a, onm_ref[...]), _rms(ob_ref[...].astype(F32), ons_ref[...])],
                          axis=1).astype(BF16)
    x1 = x_ref[...] + _dot(mix, wout_ref[...])

    hc = _rms(x1, ncross_ref[...]).astype(BF16)
    q = _dot(hc, wcq_ref[...]) * (float(X_HD) ** -0.5)
    heads = []
    for hd in range(X_HEADS):
        sl = slice(hd * X_HD, (hd + 1) * X_HD)
        s = _dot_nt(q[:, sl].astype(BF16), km_ref[0, :, sl])
        e = jnp.exp(s - jnp.max(s, axis=-1, keepdims=True))
        p = e / jnp.sum(e, axis=-1, keepdims=True)
        heads.append(_dot(p.astype(BF16), vm_ref[0, :, sl]))
    x2 = x1 + _dot(jnp.concatenate(heads, axis=1).astype(BF16), wco_ref[...])
    x2_ref[...] = x2

    hf = _rms(x2, nffn_ref[...]).astype(BF16)
    hf_ref[...] = hf
    qpT = _dot_nt(wqT_ref[...], hf)
    for hc_i in range(PEER_HEADS * 2):
        sl = slice(hc_i * PEER_HALF, (hc_i + 1) * PEER_HALF)
        scT_ref[sl, :] = _dot(keys_ref[hc_i], qpT[sl].astype(BF16))


def _mid(x2d, o_lat, o_b, k_mem, v_mem, w_uv, onm, ons, w_out, ncross, w_cq, w_co, nffn, wqT, keys2,
         S, tile):
    T, D = x2d.shape
    row = lambda i: (i, 0)
    mem_idx = lambda i: ((i * tile) // S, 0, 0)
    n_sc = PEER_HEADS * 2 * N_KEYS
    consts = [w_uv, onm, ons, w_out, ncross, w_cq, w_co, nffn, wqT, keys2]
    return pl.pallas_call(
        _mid_kernel,
        grid=(T // tile,),
        in_specs=[pl.BlockSpec((tile, D), row),
                  pl.BlockSpec((tile, o_lat.shape[1]), row),
                  pl.BlockSpec((tile, o_b.shape[1]), row),
                  pl.BlockSpec((1,) + k_mem.shape[1:], mem_idx),
                  pl.BlockSpec((1,) + v_mem.shape[1:], mem_idx)]
                 + [_const_spec(c.shape) for c in consts],
        out_specs=[pl.BlockSpec((tile, D), row), pl.BlockSpec((tile, D), row),
                   pl.BlockSpec((n_sc, tile), lambda i: (0, i))],
        out_shape=[jax.ShapeDtypeStruct((T, D), F32), jax.ShapeDtypeStruct((T, D), BF16),
                   jax.ShapeDtypeStruct((n_sc, T), F32)],
        compiler_params=_params("arbitrary"),
        name="mid",
    )(x2d, o_lat, o_b, k_mem, v_mem, *consts)


_CAND_ROWS = [(0, 0), (0, 8)] + [(a, 0) for a in range(1, 8)]
_N_CAND = 8 * (len(_CAND_ROWS) + 1)


def _topk_kernel(sc_ref, i_ref, j_ref, g_ref, is_ref, js_ref, gs_ref):
    tk = sc_ref.shape[1]
    key_iota = lax.broadcasted_iota(jnp.int32, (N_KEYS, tk), 0)
    k_iota = lax.broadcasted_iota(jnp.int32, (PEER_TOPK, tk), 0)
    sub8 = lax.broadcasted_iota(jnp.int32, (8, tk), 0)
    pos = jnp.concatenate([a * PEER_TOPK + b0 + sub8 for a, b0 in _CAND_ROWS]
                          + [(sub8 + 8) * PEER_TOPK], axis=0)

    def top_keys(vals):
        def body(k, carry):
            vals, s_out, i_out = carry
            m = jnp.max(vals, axis=0, keepdims=True)
            idx = jnp.min(jnp.where(vals == m, key_iota, N_KEYS), axis=0, keepdims=True)
            vals = jnp.where(key_iota == idx, NEG_INF, vals)
            sel = k_iota == k
            return vals, jnp.where(sel, m, s_out), jnp.where(sel, idx, i_out)

        init = (vals, jnp.zeros((PEER_TOPK, tk), F32), jnp.zeros((PEER_TOPK, tk), jnp.int32))
        _, s_out, i_out = lax.fori_loop(0, PEER_TOPK, body, init, unroll=True)
        return s_out, i_out

    def head(hd, carry):
        base = pl.multiple_of(hd * (2 * N_KEYS), 2 * N_KEYS)
        s0, i0 = top_keys(sc_ref[pl.ds(base, N_KEYS), :])
        s1, i1 = top_keys(sc_ref[pl.ds(base + N_KEYS, N_KEYS), :])
        cs = jnp.concatenate([s0[a:a + 1] + s1[b0:b0 + 8] for a, b0 in _CAND_ROWS]
                             + [s0[8:16] + s1[0:1]], axis=0)
        ci = jnp.concatenate([i0[a:a + 1] * N_KEYS + i1[b0:b0 + 8] for a, b0 in _CAND_ROWS]
                             + [i0[8:16] * N_KEYS + i1[0:1]], axis=0)

        def body(k, carry):
            cs, bs, bi = carry
            m = jnp.max(cs, axis=0, keepdims=True)
            p = jnp.min(jnp.where(cs == m, pos, PEER_TOPK * PEER_TOPK), axis=0, keepdims=True)
            hit = pos == p
            e = jnp.max(jnp.where(hit, ci, -1), axis=0, keepdims=True)
            sel = k_iota == k
            return jnp.where(hit, NEG_INF, cs), jnp.where(sel, m, bs), jnp.where(sel, e, bi)

        init = (cs, jnp.zeros((PEER_TOPK, tk), F32), jnp.zeros((PEER_TOPK, tk), jnp.int32))
        _, bs, bi = lax.fori_loop(0, PEER_TOPK, body, init, unroll=True)
        e = jnp.exp(bs - jnp.max(bs, axis=0, keepdims=True))
        g = e / jnp.sum(e, axis=0, keepdims=True)
        rows = pl.ds(pl.multiple_of(hd * PEER_TOPK, PEER_TOPK), PEER_TOPK)
        gs_ref[rows, :] = g
        is_ref[rows, :] = (bi >> 7).astype(F32)
        js_ref[rows, :] = (bi & (N_KEYS - 1)).astype(F32)
        return carry

    lax.fori_loop(0, PEER_HEADS, head, 0)
    for c in range(tk // LANES):
        cols = slice(c * LANES, (c + 1) * LANES)
        i_ref[cols, :] = is_ref[:, cols].T
        j_ref[cols, :] = js_ref[:, cols].T
        g_ref[cols, :] = gs_ref[:, cols].T


def _topk(scT, tk):
    n_sc, T = scT.shape
    n_sel = PEER_HEADS * PEER_TOPK
    out = jax.ShapeDtypeStruct((T, n_sel), F32)
    return pl.pallas_call(
        _topk_kernel,
        grid=(T // tk,),
        in_specs=[pl.BlockSpec((n_sc, tk), lambda i: (0, i))],
        out_specs=[pl.BlockSpec((tk, n_sel), lambda i: (i, 0))] * 3,
        out_shape=[out] * 3,
        scratch_shapes=[pltpu.VMEM((n_sel, tk), F32)] * 3,
        compiler_params=_params("arbitrary"),
        name="topk",
    )(scT)


def _peer_kernel(hf_ref, i_ref, j_ref, g_ref, uT_ref, v_ref, x2_ref, nf_ref, o_ref, w_ref, y_ref,
                 *, tile, n_blk):
    p = pl.program_id(1)

    @pl.when(p == 0)
    def _():
        sub = lax.broadcasted_iota(jnp.int32, (N_KEYS, N_KEYS), 0).astype(F32)

        def tok(t, carry):
            row = pl.ds(t, 1)
            a_t = jnp.where(sub == i_ref[row, :], g_ref[row, :], 0.0).astype(BF16)
            b_t = jnp.where(sub == j_ref[row, :], 1.0, 0.0).astype(BF16)
            w_ref[pl.ds(pl.multiple_of(t * N_KEYS, N_KEYS), N_KEYS), :] = _dot_nt(a_t, b_t)
            return carry

        lax.fori_loop(0, tile, tok, 0)
        y_ref[...] = jnp.zeros(y_ref.shape, F32)

    act = _dot(hf_ref[...], uT_ref[...])
    gated = []
    for c in range(n_blk):
        a = act[:, c * N_KEYS:(c + 1) * N_KEYS]
        w = w_ref[pl.ds(p * n_blk + c, tile, stride=N_KEYS), :]
        gelu = 0.5 * a * (1.0 + lax.erf(a * (0.5 ** 0.5)))
        gated.append((gelu * w).astype(BF16))
    y_ref[...] += _dot(jnp.concatenate(gated, axis=1), v_ref[...])

    @pl.when(p == pl.num_programs(1) - 1)
    def _():
        o_ref[...] = _rms(x2_ref[...] + y_ref[...], nf_ref[...])


def _peer(hf, sel_i, sel_j, sel_g, uT, v, x2, norm_final, tile, n_blk):
    T, D = x2.shape
    n_exp = v.shape[0]
    n_sel = sel_i.shape[1]
    ew = n_blk * N_KEYS
    row = lambda t, p: (t, 0)
    return pl.pallas_call(
        functools.partial(_peer_kernel, tile=tile, n_blk=n_blk),
        grid=(T // tile, n_exp // ew),
        in_specs=[pl.BlockSpec((tile, D), row),
                  pl.BlockSpec((tile, n_sel), row), pl.BlockSpec((tile, n_sel), row),
                  pl.BlockSpec((tile, n_sel), row),
                  pl.BlockSpec((D, ew), lambda t, p: (0, p)),
                  pl.BlockSpec((ew, D), lambda t, p: (p, 0)),
                  pl.BlockSpec((tile, D), row),
                  _const_spec(norm_final.shape)],
        out_specs=pl.BlockSpec((tile, D), row),
        out_shape=jax.ShapeDtypeStruct((T, D), F32),
        scratch_shapes=[pltpu.VMEM((tile * N_KEYS, N_KEYS), F32), pltpu.VMEM((tile, D), F32)],
        compiler_params=_params("arbitrary", "arbitrary"),
        name="peer",
    )(hf, sel_i, sel_j, sel_g, uT, v, x2, norm_final)


def _tile(n, want):
    t = min(n, want)
    assert n % t == 0, (n, t)
    return t


def _layer(x2d, mem, pos2d, B, S, norm_mix, w_in, q_a_norm, w_q_b, kv_a_norm, w_kv_b, swa_sinks,
           out_norm_mla, out_norm_swa, w_out, norm_cross, norm_mem, w_cq, w_ck, w_cv, w_co,
           norm_ffn, peer_w_q, peer_keys, peer_u, peer_v):
    D = x2d.shape[1]
    row = lambda g: g.reshape(1, -1)

    o = MLA_Q_RANK + MLA_KV_RANK
    w_kr = w_in[:, o:o + MLA_ROPE]
    o += MLA_ROPE
    w_qs = w_in[:, o:o + SWA_HEADS * SWA_HD].reshape(D, SWA_HEADS, SWA_HD)
    o += SWA_HEADS * SWA_HD
    w_ks = w_in[:, o:o + SWA_KV_HEADS * SWA_HD]
    o += SWA_KV_HEADS * SWA_HD
    w_vs = w_in[:, o:o + SWA_KV_HEADS * SWA_HD]
    group = SWA_HEADS // SWA_KV_HEADS
    zeros = jnp.zeros((D, SWA_HD), w_in.dtype)
    qs_slots = [jnp.concatenate([w_qs[:, hh], zeros] if hh // group == 0 else [zeros, w_qs[:, hh]], axis=1)
                for hh in range(SWA_HEADS)]
    half = MLA_ROPE // 2
    swap = lambda w: jnp.concatenate([w[:, half:], w[:, :half]], axis=1)
    w_in_r = jnp.concatenate([w_in[:, :MLA_Q_RANK + MLA_KV_RANK]] + qs_slots
                             + [w_ks, w_vs, w_kr, swap(w_kr)], axis=1).astype(BF16)

    wq = w_q_b.reshape(MLA_Q_RANK, MLA_HEADS, MLA_NOPE + MLA_ROPE)
    q_nope = wq[:, :, :MLA_NOPE].reshape(MLA_Q_RANK, MLA_HEADS * MLA_NOPE)
    q_rope = [jnp.concatenate([wq[:, hd, MLA_NOPE:], swap(wq[:, hd, MLA_NOPE:])], axis=1)
              for hd in range(MLA_HEADS)]
    w_qb_r = jnp.concatenate([q_nope] + q_rope, axis=1).astype(BF16)

    wkv = w_kv_b.reshape(MLA_KV_RANK, MLA_HEADS, MLA_NOPE + MLA_V)
    w_ukT = jnp.transpose(wkv[:, :, :MLA_NOPE], (1, 2, 0)).astype(BF16)
    w_uv = jnp.transpose(wkv[:, :, MLA_NOPE:], (1, 0, 2)).astype(BF16)

    inv = ROPE_THETA ** (-jnp.arange(half, dtype=F32) / half)
    inv_slot = jnp.tile(inv, 2 * MLA_ROPE // half).reshape(1, 2 * MLA_ROPE)
    sgn_slot = jnp.concatenate([jnp.ones((MLA_ROPE,), F32), -jnp.ones((half,), F32),
                                jnp.ones((half,), F32)]).reshape(1, 2 * MLA_ROPE)

    w_ckv = jnp.concatenate([w_ck, w_cv], axis=1).astype(BF16)
    wqT = peer_w_q.T.astype(BF16)
    keys2 = peer_keys.reshape(PEER_HEADS * 2, N_KEYS, PEER_HALF).astype(BF16)
    uT = peer_u.T.astype(BF16)
    v_b = peer_v.astype(BF16)

    k_mem, v_mem = _mem_kv(mem, row(norm_mem), w_ckv)
    qcat, kcat, qs, ks, vs = _proj(x2d, pos2d, inv_slot, sgn_slot, row(norm_mix), w_in_r,
                                   row(q_a_norm), w_qb_r, row(kv_a_norm), w_ukT, _tile(S, 512))
    o_lat = _mla(qcat, kcat.reshape(B, S, MLA_QK), _tile(S, 256), _tile(S, 512))
    o_b = _swa(swa_sinks, qs, ks, vs, B, S, _tile(S, 512))
    x2, hf, scT = _mid(x2d, o_lat, o_b, k_mem, v_mem, w_uv, row(out_norm_mla), row(out_norm_swa),
                       w_out.astype(BF16), row(norm_cross), w_cq.astype(BF16), w_co.astype(BF16),
                       row(norm_ffn), wqT, keys2, S, _tile(S, 256))
    sel_i, sel_j, sel_g = _topk(scT, _tile(S, 128))
    return hf, sel_i, sel_j, sel_g, uT, v_b, x2


def kernel(x, mem, positions, norm_mix, w_in, q_a_norm, w_q_b, kv_a_norm, w_kv_b, swa_sinks,
           out_norm_mla, out_norm_swa, w_out, norm_cross, norm_mem, w_cq, w_ck, w_cv, w_co,
           norm_ffn, peer_w_q, peer_keys, peer_u, peer_v, norm_final):
    B, S, D = x.shape
    depth = norm_mix.shape[0]
    assert depth == 1, "the final rmsnorm is fused into the last layer's PEER kernel"
    x2d = x.reshape(B * S, D)
    pos2d = positions.reshape(B * S, 1)
    l = 0
    hf, sel_i, sel_j, sel_g, uT, v_b, x2 = _layer(
        x2d, mem, pos2d, B, S, norm_mix[l], w_in[l], q_a_norm[l], w_q_b[l], kv_a_norm[l], w_kv_b[l],
        swa_sinks[l], out_norm_mla[l], out_norm_swa[l], w_out[l], norm_cross[l], norm_mem[l],
        w_cq[l], w_ck[l], w_cv[l], w_co[l], norm_ffn[l], peer_w_q[l], peer_keys[l], peer_u[l],
        peer_v[l])
    out = _peer(hf, sel_i, sel_j, sel_g, uT, v_b, x2, norm_final.reshape(1, D), _tile(S, 256), 4)
    return out.reshape(B, S, D)
```

```python
import functools
import math

import jax
import jax.numpy as jnp
from jax import lax
from jax.experimental import pallas as pl
from jax.experimental.pallas import tpu as pltpu

F32 = jnp.float32
BF16 = jnp.bfloat16
NEG_INF = float("-inf")
LOG2E = math.log2(math.e)

EPS = 1e-6
ROPE_THETA = 10000.0
MLA_HEADS = 4
MLA_NOPE = 128
MLA_ROPE = 64
MLA_V = 128
MLA_Q_RANK = 256
MLA_KV_RANK = 128
MLA_QK = MLA_KV_RANK + MLA_ROPE
SWA_HEADS = 8
SWA_KV_HEADS = 2
SWA_HD = 64
SWA_BLK = 128
X_HEADS = 4
X_HD = 128
PEER_HEADS = 8
N_KEYS = 128
PEER_HALF = 128
PEER_TOPK = 16

LANES = 128
VMEM_LIMIT = 48 * 1024 * 1024
PEER_VMEM_LIMIT = 56 * 1024 * 1024


def _rms(x, g):
    return x * lax.rsqrt(jnp.mean(x * x, axis=-1, keepdims=True) + EPS) * g


def _dot(a, b):
    return jnp.dot(a, b, preferred_element_type=F32)


def _dot_nt(a, b):
    return lax.dot_general(a, b, (((1,), (1,)), ((), ())), preferred_element_type=F32)


def _const_spec(shape):
    zeros = (0,) * len(shape)
    return pl.BlockSpec(shape, lambda *_: zeros)


def _params(*sem):
    return pltpu.CompilerParams(dimension_semantics=sem, vmem_limit_bytes=VMEM_LIMIT)


def _memkv_kernel(mem_ref, g_ref, w_ref, k_ref, v_ref):
    mn = _rms(mem_ref[0], g_ref[...]).astype(BF16)
    kv = _dot(mn, w_ref[...])
    width = k_ref.shape[-1]
    k_ref[0] = kv[:, :width].astype(BF16)
    v_ref[0] = kv[:, width:].astype(BF16)


def _mem_kv(mem, norm_mem, w_ckv):
    B, M, D = mem.shape
    width = w_ckv.shape[1] // 2
    return pl.pallas_call(
        _memkv_kernel,
        grid=(B,),
        in_specs=[pl.BlockSpec((1, M, D), lambda b: (b, 0, 0)),
                  _const_spec((1, D)), _const_spec(w_ckv.shape)],
        out_specs=[pl.BlockSpec((1, M, width), lambda b: (b, 0, 0))] * 2,
        out_shape=[jax.ShapeDtypeStruct((B, M, width), BF16)] * 2,
        compiler_params=_params("arbitrary"),
        name="mem_kv",
    )(mem, norm_mem, w_ckv)


_C_CQ = 0
_C_CKV = _C_CQ + MLA_Q_RANK
_C_QS = _C_CKV + MLA_KV_RANK
_C_KS = _C_QS + SWA_HEADS * LANES
_C_VS = _C_KS + SWA_KV_HEADS * SWA_HD
_C_KR = _C_VS + SWA_KV_HEADS * SWA_HD
_C_END = _C_KR + 2 * MLA_ROPE


def _proj_kernel(x_ref, pos_ref, inv_ref, sgn_ref, nmix_ref, win_ref, qan_ref, wqb_ref, kvan_ref,
                 wuk_ref, qcat_ref, kcat_ref, ct_ref, qs_ref, ks_ref, vs_ref):
    h = _rms(x_ref[...], nmix_ref[...]).astype(BF16)
    proj = _dot(h, win_ref[...])
    qs_ref[...] = proj[:, _C_QS:_C_KS].astype(BF16)
    ks_ref[...] = proj[:, _C_KS:_C_VS].astype(BF16)
    vs_ref[...] = proj[:, _C_VS:_C_KR].astype(BF16)

    ang = pos_ref[...].astype(F32) * inv_ref[...]
    lane = lax.broadcasted_iota(jnp.int32, ang.shape, 1)
    cs = jnp.where(lane < MLA_ROPE, jnp.cos(ang), jnp.sin(ang) * sgn_ref[...])

    def rope_slot(slot):
        r = slot * cs
        return (r + pltpu.roll(r, MLA_ROPE, 1))[:, :MLA_ROPE]

    c = _rms(proj[:, _C_CKV:_C_QS], kvan_ref[...])
    k_r = rope_slot(proj[:, _C_KR:_C_END])
    kcat_ref[...] = jnp.concatenate([c, k_r], axis=1).astype(BF16)
    ct_ref[0] = c.T.astype(BF16)

    qn = _rms(proj[:, _C_CQ:_C_CKV], qan_ref[...]).astype(BF16)
    q2 = _dot(qn, wqb_ref[...])
    scale = float(MLA_NOPE + MLA_ROPE) ** -0.5 * LOG2E
    rope_base = MLA_HEADS * MLA_NOPE
    for hd in range(MLA_HEADS):
        q_lat = _dot(q2[:, hd * MLA_NOPE:(hd + 1) * MLA_NOPE].astype(BF16), wuk_ref[hd])
        q_r = rope_slot(q2[:, rope_base + hd * LANES: rope_base + (hd + 1) * LANES])
        qcat_ref[hd] = (jnp.concatenate([q_lat, q_r], axis=1) * scale).astype(BF16)


def _proj(x2d, pos2d, inv_slot, sgn_slot, norm_mix, w_in_r, q_a_norm, w_qb_r, kv_a_norm, w_ukT, tile):
    T, D = x2d.shape
    qs_w = SWA_HEADS * LANES
    kv_w = SWA_KV_HEADS * SWA_HD
    row = lambda i: (i, 0)
    return pl.pallas_call(
        _proj_kernel,
        grid=(T // tile,),
        in_specs=[pl.BlockSpec((tile, D), row), pl.BlockSpec((tile, 1), row),
                  _const_spec(inv_slot.shape), _const_spec(sgn_slot.shape),
                  _const_spec(norm_mix.shape), _const_spec(w_in_r.shape),
                  _const_spec(q_a_norm.shape), _const_spec(w_qb_r.shape),
                  _const_spec(kv_a_norm.shape), _const_spec(w_ukT.shape)],
        out_specs=[pl.BlockSpec((MLA_HEADS, tile, MLA_QK), lambda i: (0, i, 0)),
                   pl.BlockSpec((tile, MLA_QK), row),
                   pl.BlockSpec((1, MLA_KV_RANK, tile), lambda i: (i, 0, 0)),
                   pl.BlockSpec((tile, qs_w), row),
                   pl.BlockSpec((tile, kv_w), row),
                   pl.BlockSpec((tile, kv_w), row)],
        out_shape=[jax.ShapeDtypeStruct((MLA_HEADS, T, MLA_QK), BF16),
                   jax.ShapeDtypeStruct((T, MLA_QK), BF16),
                   jax.ShapeDtypeStruct((T // tile, MLA_KV_RANK, tile), BF16),
                   jax.ShapeDtypeStruct((T, qs_w), BF16),
                   jax.ShapeDtypeStruct((T, kv_w), BF16),
                   jax.ShapeDtypeStruct((T, kv_w), BF16)],
        compiler_params=_params("arbitrary"),
        name="proj",
    )(x2d, pos2d, inv_slot, sgn_slot, norm_mix, w_in_r, q_a_norm, w_qb_r, kv_a_norm, w_ukT)


def _mla_kernel(q_ref, k_ref, ct_ref, o_ref, sa_ref, sb_ref, m_ref, l_ref, acc_ref, *, tq, kb):
    i = pl.program_id(1)
    rows = MLA_HEADS * tq
    q = q_ref[...].reshape(rows, MLA_QK)
    m_ref[...] = jnp.full(m_ref.shape, NEG_INF, F32)
    l_ref[...] = jnp.zeros(l_ref.shape, F32)
    acc_ref[...] = jnp.zeros(acc_ref.shape, F32)

    def scores(j, s_ref):
        s_ref[...] = _dot_nt(k_ref[j], q)

    def update(j, s_ref, masked):
        s = s_ref[...]
        if masked:
            q_idx = i * tq + (lax.broadcasted_iota(jnp.int32, s.shape, 1) & (tq - 1))
            k_idx = j * kb + lax.broadcasted_iota(jnp.int32, s.shape, 0)
            s = jnp.where(k_idx <= q_idx, s, NEG_INF)
        m_prev = m_ref[...]
        m_new = jnp.maximum(m_prev, jnp.max(s, axis=0, keepdims=True))
        alpha = jnp.exp2(m_prev - m_new)
        p = jnp.exp2(s - m_new)
        l_ref[...] = alpha * l_ref[...] + jnp.sum(p, axis=0, keepdims=True)
        acc_ref[...] = alpha * acc_ref[...] + _dot(ct_ref[j], p.astype(BF16))
        m_ref[...] = m_new

    n_full = (i * tq) // kb
    scores(0, sa_ref)

    def body(t, carry):
        j = 2 * t
        scores(j + 1, sb_ref)
        update(j, sa_ref, False)
        scores(j + 2, sa_ref)
        update(j + 1, sb_ref, False)
        return carry

    lax.fori_loop(0, n_full // 2, body, 0)

    @pl.when(n_full % 2 == 0)
    def _():
        update(n_full, sa_ref, True)

    @pl.when(n_full % 2 == 1)
    def _():
        scores(n_full, sb_ref)
        update(n_full - 1, sa_ref, False)
        update(n_full, sb_ref, True)

    o = acc_ref[...] / l_ref[...]
    for hd in range(MLA_HEADS):
        o_ref[:, hd * MLA_KV_RANK:(hd + 1) * MLA_KV_RANK] = o[:, hd * tq:(hd + 1) * tq].T.astype(BF16)


def _mla(qcat, kcat_blk, ct_blk, B, tq):
    nkb, kb, _ = kcat_blk.shape
    S = nkb * kb // B
    nq = S // tq
    return pl.pallas_call(
        functools.partial(_mla_kernel, tq=tq, kb=kb),
        grid=(B, nq),
        in_specs=[pl.BlockSpec((MLA_HEADS, tq, MLA_QK), lambda b, i: (0, b * nq + i, 0)),
                  pl.BlockSpec((nkb // B, kb, MLA_QK), lambda b, i: (b, 0, 0)),
                  pl.BlockSpec((nkb // B, MLA_KV_RANK, kb), lambda b, i: (b, 0, 0))],
        out_specs=pl.BlockSpec((tq, MLA_HEADS * MLA_KV_RANK), lambda b, i: (b * nq + i, 0)),
        out_shape=jax.ShapeDtypeStruct((B * S, MLA_HEADS * MLA_KV_RANK), BF16),
        scratch_shapes=[pltpu.VMEM((kb, MLA_HEADS * tq), F32), pltpu.VMEM((kb, MLA_HEADS * tq), F32),
                        pltpu.VMEM((1, MLA_HEADS * tq), F32), pltpu.VMEM((1, MLA_HEADS * tq), F32),
                        pltpu.VMEM((MLA_KV_RANK, MLA_HEADS * tq), F32)],
        compiler_params=_params("arbitrary", "arbitrary"),
        name="mla",
    )(qcat, kcat_blk, ct_blk)


def _swa_kernel(sink_ref, q_ref, k_ref, kp_ref, v_ref, vp_ref, o_ref, kf_ref, vf_ref, *, ts):
    i = pl.program_id(1)
    kf_ref[0:SWA_BLK] = kp_ref[...]
    kf_ref[SWA_BLK:] = k_ref[...]
    vf_ref[0:SWA_BLK] = vp_ref[...]
    vf_ref[SWA_BLK:] = v_ref[...]
    a = lax.broadcasted_iota(jnp.int32, (SWA_BLK, SWA_BLK), 0)
    j = lax.broadcasted_iota(jnp.int32, (SWA_BLK, SWA_BLK), 1)
    dist_cur = (a - j).astype(F32)
    dist_prev = dist_cur + float(SWA_BLK)
    valid_cur = j <= a
    valid_prev = j > a
    low_half = j < SWA_HD
    scale = float(SWA_HD) ** -0.5
    group = SWA_HEADS // SWA_KV_HEADS
    for n in range(ts // SWA_BLK):
        r0 = n * SWA_BLK
        k_prev = kf_ref[r0:r0 + SWA_BLK]
        k_cur = kf_ref[r0 + SWA_BLK:r0 + 2 * SWA_BLK]
        v_prev = vf_ref[r0:r0 + SWA_BLK]
        v_cur = vf_ref[r0 + SWA_BLK:r0 + 2 * SWA_BLK]
        prev_pen = jnp.where(jnp.logical_and(i == 0, n == 0), NEG_INF, 0.0).astype(F32)
        outs = []
        for hh in range(SWA_HEADS):
            slope = 2.0 ** (-(8.0 / SWA_HEADS) * (hh + 1))
            qh = q_ref[r0:r0 + SWA_BLK, hh * LANES:(hh + 1) * LANES]
            sp = _dot_nt(qh, k_prev) * scale - slope * dist_prev
            sc = _dot_nt(qh, k_cur) * scale - slope * dist_cur
            sp = jnp.where(valid_prev, sp, NEG_INF) + prev_pen
            sc = jnp.where(valid_cur, sc, NEG_INF)
            sink = sink_ref[hh]
            m = jnp.maximum(jnp.maximum(jnp.max(sp, axis=-1, keepdims=True),
                                        jnp.max(sc, axis=-1, keepdims=True)), sink)
            ep = jnp.exp(sp - m)
            ec = jnp.exp(sc - m)
            den = (jnp.sum(ep, axis=-1, keepdims=True) + jnp.sum(ec, axis=-1, keepdims=True)
                   + jnp.exp(sink - m))
            o = (_dot(ep.astype(BF16), v_prev) + _dot(ec.astype(BF16), v_cur)) / den
            outs.append(o)
        for pair in range(SWA_HEADS // 2):
            kv = (2 * pair) // group
            oe, oo = outs[2 * pair], outs[2 * pair + 1]
            if kv == 0:
                both = jnp.where(low_half, oe, pltpu.roll(oo, SWA_HD, 1))
            else:
                both = jnp.where(low_half, pltpu.roll(oe, SWA_HD, 1), oo)
            o_ref[r0:r0 + SWA_BLK, pair * LANES:(pair + 1) * LANES] = both.astype(BF16)


def _swa(sinks, qs, ks, vs, B, S, ts):
    T = B * S
    nt = S // ts
    kv_w = SWA_KV_HEADS * SWA_HD
    blk_per_tile = ts // SWA_BLK
    cur = lambda b, i: (b * nt + i, 0)
    prev = lambda b, i: (jnp.maximum((b * nt + i) * blk_per_tile - 1, 0), 0)
    return pl.pallas_call(
        functools.partial(_swa_kernel, ts=ts),
        grid=(B, nt),
        in_specs=[pl.BlockSpec(memory_space=pltpu.SMEM),
                  pl.BlockSpec((ts, SWA_HEADS * LANES), cur),
                  pl.BlockSpec((ts, kv_w), cur), pl.BlockSpec((SWA_BLK, kv_w), prev),
                  pl.BlockSpec((ts, kv_w), cur), pl.BlockSpec((SWA_BLK, kv_w), prev)],
        out_specs=pl.BlockSpec((ts, SWA_HEADS * SWA_HD), cur),
        out_shape=jax.ShapeDtypeStruct((T, SWA_HEADS * SWA_HD), BF16),
        scratch_shapes=[pltpu.VMEM((ts + SWA_BLK, kv_w), BF16), pltpu.VMEM((ts + SWA_BLK, kv_w), BF16)],
        compiler_params=_params("arbitrary", "arbitrary"),
        name="swa",
    )(sinks, qs, ks, ks, vs, vs)


def _mid_kernel(x_ref, ol_ref, ob_ref, km_ref, vm_ref, wuv_ref, onm_ref, ons_ref, wout_ref,
                ncross_ref, wcq_ref, wco_ref, nffn_ref, wqT_ref, keys_ref,
                x2_ref, hf_ref, scT_ref):
    ol = ol_ref[...]
    o_a = jnp.concatenate(
        [_dot(ol[:, hd * MLA_KV_RANK:(hd + 1) * MLA_KV_RANK], wuv_ref[hd]) for hd in range(MLA_HEADS)],
        axis=1)
    mix = jnp.concatenate([_rms(o_a, onm_ref[...]), _rms(ob_ref[...].astype(F32), ons_ref[...])],
                          axis=1).astype(BF16)
    x1 = x_ref[...] + _dot(mix, wout_ref[...])

    hc = _rms(x1, ncross_ref[...]).astype(BF16)
    q = _dot(hc, wcq_ref[...]) * (float(X_HD) ** -0.5)
    heads = []
    for hd in range(X_HEADS):
        sl = slice(hd * X_HD, (hd + 1) * X_HD)
        s = _dot_nt(q[:, sl].astype(BF16), km_ref[0, :, sl])
        e = jnp.exp(s - jnp.max(s, axis=-1, keepdims=True))
        p = e / jnp.sum(e, axis=-1, keepdims=True)
        heads.append(_dot(p.astype(BF16), vm_ref[0, :, sl]))
    x2 = x1 + _dot(jnp.concatenate(heads, axis=1).astype(BF16), wco_ref[...])
    x2_ref[...] = x2

    hf = _rms(x2, nffn_ref[...]).astype(BF16)
    hf_ref[...] = hf
    qpT = _dot_nt(wqT_ref[...], hf)
    for hc_i in range(PEER_HEADS * 2):
        sl = slice(hc_i * PEER_HALF, (hc_i + 1) * PEER_HALF)
        scT_ref[sl, :] = _dot(keys_ref[hc_i], qpT[sl].astype(BF16))


def _mid(x2d, o_lat, o_b, k_mem, v_mem, w_uv, onm, ons, w_out, ncross, w_cq, w_co, nffn, wqT, keys2,
         S, tile):
    T, D = x2d.shape
    row = lambda i: (i, 0)
    mem_idx = lambda i: ((i * tile) // S, 0, 0)
    n_sc = PEER_HEADS * 2 * N_KEYS
    consts = [w_uv, onm, ons, w_out, ncross, w_cq, w_co, nffn, wqT, keys2]
    return pl.pallas_call(
        _mid_kernel,
        grid=(T // tile,),
        in_specs=[pl.BlockSpec((tile, D), row),
                  pl.BlockSpec((tile, o_lat.shape[1]), row),
                  pl.BlockSpec((tile, o_b.shape[1]), row),
                  pl.BlockSpec((1,) + k_mem.shape[1:], mem_idx),
                  pl.BlockSpec((1,) + v_mem.shape[1:], mem_idx)]
                 + [_const_spec(c.shape) for c in consts],
        out_specs=[pl.BlockSpec((tile, D), row), pl.BlockSpec((tile, D), row),
                   pl.BlockSpec((n_sc, tile), lambda i: (0, i))],
        out_shape=[jax.ShapeDtypeStruct((T, D), F32), jax.ShapeDtypeStruct((T, D), BF16),
                   jax.ShapeDtypeStruct((n_sc, T), F32)],
        compiler_params=_params("arbitrary"),
        name="mid",
    )(x2d, o_lat, o_b, k_mem, v_mem, *consts)


_CAND_ROWS = [(0, 0), (0, 8)] + [(a, 0) for a in range(1, 8)]
_N_CAND = 8 * (len(_CAND_ROWS) + 1)


def _topk_kernel(sc_ref, i_ref, j_ref, g_ref, is_ref, js_ref, gs_ref):
    tk = sc_ref.shape[1]
    key_iota = lax.broadcasted_iota(jnp.int32, (N_KEYS, tk), 0)
    k_iota = lax.broadcasted_iota(jnp.int32, (PEER_TOPK, tk), 0)
    sub8 = lax.broadcasted_iota(jnp.int32, (8, tk), 0)
    pos = jnp.concatenate([a * PEER_TOPK + b0 + sub8 for a, b0 in _CAND_ROWS]
                          + [(sub8 + 8) * PEER_TOPK], axis=0)

    def top_keys(vals):
        def body(k, carry):
            vals, s_out, i_out = carry
            m = jnp.max(vals, axis=0, keepdims=True)
            idx = jnp.min(jnp.where(vals == m, key_iota, N_KEYS), axis=0, keepdims=True)
            vals = jnp.where(key_iota == idx, NEG_INF, vals)
            sel = k_iota == k
            return vals, jnp.where(sel, m, s_out), jnp.where(sel, idx, i_out)

        init = (vals, jnp.zeros((PEER_TOPK, tk), F32), jnp.zeros((PEER_TOPK, tk), jnp.int32))
        _, s_out, i_out = lax.fori_loop(0, PEER_TOPK, body, init, unroll=True)
        return s_out, i_out

    def head(hd, carry):
        base = pl.multiple_of(hd * (2 * N_KEYS), 2 * N_KEYS)
        s0, i0 = top_keys(sc_ref[pl.ds(base, N_KEYS), :])
        s1, i1 = top_keys(sc_ref[pl.ds(base + N_KEYS, N_KEYS), :])
        cs = jnp.concatenate([s0[a:a + 1] + s1[b0:b0 + 8] for a, b0 in _CAND_ROWS]
                             + [s0[8:16] + s1[0:1]], axis=0)
        ci = jnp.concatenate([i0[a:a + 1] * N_KEYS + i1[b0:b0 + 8] for a, b0 in _CAND_ROWS]
                             + [i0[8:16] * N_KEYS + i1[0:1]], axis=0)

        def body(k, carry):
            cs, bs, bi = carry
            m = jnp.max(cs, axis=0, keepdims=True)
            p = jnp.min(jnp.where(cs == m, pos, PEER_TOPK * PEER_TOPK), axis=0, keepdims=True)
            hit = pos == p
            e = jnp.max(jnp.where(hit, ci, -1), axis=0, keepdims=True)
            sel = k_iota == k
            return jnp.where(hit, NEG_INF, cs), jnp.where(sel, m, bs), jnp.where(sel, e, bi)

        init = (cs, jnp.zeros((PEER_TOPK, tk), F32), jnp.zeros((PEER_TOPK, tk), jnp.int32))
        _, bs, bi = lax.fori_loop(0, PEER_TOPK, body, init, unroll=True)
        e = jnp.exp(bs - jnp.max(bs, axis=0, keepdims=True))
        g = e / jnp.sum(e, axis=0, keepdims=True)
        rows = pl.ds(pl.multiple_of(hd * PEER_TOPK, PEER_TOPK), PEER_TOPK)
        gs_ref[rows, :] = g
        is_ref[rows, :] = (bi >> 7).astype(F32)
        js_ref[rows, :] = (bi & (N_KEYS - 1)).astype(F32)
        return carry

    lax.fori_loop(0, PEER_HEADS, head, 0)
    for c in range(tk // LANES):
        cols = slice(c * LANES, (c + 1) * LANES)
        i_ref[cols, :] = is_ref[:, cols].T
        j_ref[cols, :] = js_ref[:, cols].T
        g_ref[cols, :] = gs_ref[:, cols].T


def _topk(scT, tk):
    n_sc, T = scT.shape
    n_sel = PEER_HEADS * PEER_TOPK
    out = jax.ShapeDtypeStruct((T, n_sel), F32)
    return pl.pallas_call(
        _topk_kernel,
        grid=(T // tk,),
        in_specs=[pl.BlockSpec((n_sc, tk), lambda i: (0, i))],
        out_specs=[pl.BlockSpec((tk, n_sel), lambda i: (i, 0))] * 3,
        out_shape=[out] * 3,
        scratch_shapes=[pltpu.VMEM((n_sel, tk), F32)] * 3,
        compiler_params=_params("arbitrary"),
        name="topk",
    )(scT)


_W_PAD = 8
_I_SPLIT = 2


def _peer_kernel(hf_ref, i_ref, j_ref, g_ref, uT_ref, v_ref, x2_ref, nf_ref, o_ref, w_ref, y_ref,
                 *, tile, n_blk):
    half = pl.program_id(1)
    s = pl.program_id(2)
    pitch = tile + _W_PAD
    n_i = N_KEYS // _I_SPLIT

    @pl.when(s == 0)
    def _():
        sub_i = (lax.broadcasted_iota(jnp.int32, (n_i, N_KEYS), 0) + half * n_i).astype(F32)
        sub_j = lax.broadcasted_iota(jnp.int32, (N_KEYS, N_KEYS), 0).astype(F32)

        def tok(t, carry):
            row = pl.ds(t, 1)
            a_t = jnp.where(sub_i == i_ref[row, :], 0.5 * g_ref[row, :], 0.0).astype(BF16)
            b_t = jnp.where(sub_j == j_ref[row, :], 1.0, 0.0).astype(BF16)
            w_ref[pl.ds(t, n_i, stride=pitch), :] = _dot_nt(a_t, b_t)
            return carry

        lax.fori_loop(0, tile, tok, 0, unroll=32)

    @pl.when(jnp.logical_and(half == 0, s == 0))
    def _():
        y_ref[...] = jnp.zeros(y_ref.shape, F32)

    act = _dot(hf_ref[...], uT_ref[...])
    gated = []
    for c in range(n_blk):
        a = act[:, c * N_KEYS:(c + 1) * N_KEYS]
        w = w_ref[pl.ds(pl.multiple_of((s * n_blk + c) * pitch, 8), tile), :]
        gated.append((a * (1.0 + lax.erf(a * (0.5 ** 0.5))) * w).astype(BF16))
    y_ref[...] += _dot(jnp.concatenate(gated, axis=1), v_ref[...])

    @pl.when(jnp.logical_and(half == _I_SPLIT - 1, s == pl.num_programs(2) - 1))
    def _():
        o_ref[...] = _rms(x2_ref[...] + y_ref[...], nf_ref[...])


def _peer(hf, sel_i, sel_j, sel_g, uT, v, x2, norm_final, tile, n_blk):
    T, D = x2.shape
    n_exp = v.shape[0]
    n_sel = sel_i.shape[1]
    ew = n_blk * N_KEYS
    steps = n_exp // ew // _I_SPLIT
    assert steps * n_blk * _I_SPLIT == N_KEYS
    row = lambda t, h, s: (t, 0)
    params = pltpu.CompilerParams(dimension_semantics=("arbitrary",) * 3,
                                  vmem_limit_bytes=PEER_VMEM_LIMIT)
    return pl.pallas_call(
        functools.partial(_peer_kernel, tile=tile, n_blk=n_blk),
        grid=(T // tile, _I_SPLIT, steps),
        in_specs=[pl.BlockSpec((tile, D), row),
                  pl.BlockSpec((tile, n_sel), row), pl.BlockSpec((tile, n_sel), row),
                  pl.BlockSpec((tile, n_sel), row),
                  pl.BlockSpec((D, ew), lambda t, h, s: (0, h * steps + s)),
                  pl.BlockSpec((ew, D), lambda t, h, s: (h * steps + s, 0)),
                  pl.BlockSpec((tile, D), row),
                  _const_spec(norm_final.shape)],
        out_specs=pl.BlockSpec((tile, D), row),
        out_shape=jax.ShapeDtypeStruct((T, D), F32),
        scratch_shapes=[pltpu.VMEM((N_KEYS // _I_SPLIT * (tile + _W_PAD), N_KEYS), F32),
                        pltpu.VMEM((tile, D), F32)],
        compiler_params=params,
        name="peer",
    )(hf, sel_i, sel_j, sel_g, uT, v, x2, norm_final)


def _tile(n, want):
    t = min(n, want)
    assert n % t == 0, (n, t)
    return t


def _layer(x2d, mem, pos2d, B, S, norm_mix, w_in, q_a_norm, w_q_b, kv_a_norm, w_kv_b, swa_sinks,
           out_norm_mla, out_norm_swa, w_out, norm_cross, norm_mem, w_cq, w_ck, w_cv, w_co,
           norm_ffn, peer_w_q, peer_keys, peer_u, peer_v):
    D = x2d.shape[1]
    row = lambda g: g.reshape(1, -1)

    o = MLA_Q_RANK + MLA_KV_RANK
    w_kr = w_in[:, o:o + MLA_ROPE]
    o += MLA_ROPE
    w_qs = w_in[:, o:o + SWA_HEADS * SWA_HD].reshape(D, SWA_HEADS, SWA_HD)
    o += SWA_HEADS * SWA_HD
    w_ks = w_in[:, o:o + SWA_KV_HEADS * SWA_HD]
    o += SWA_KV_HEADS * SWA_HD
    w_vs = w_in[:, o:o + SWA_KV_HEADS * SWA_HD]
    group = SWA_HEADS // SWA_KV_HEADS
    zeros = jnp.zeros((D, SWA_HD), w_in.dtype)
    qs_slots = [jnp.concatenate([w_qs[:, hh], zeros] if hh // group == 0 else [zeros, w_qs[:, hh]], axis=1)
                for hh in range(SWA_HEADS)]
    half = MLA_ROPE // 2
    swap = lambda w: jnp.concatenate([w[:, half:], w[:, :half]], axis=1)
    w_in_r = jnp.concatenate([w_in[:, :MLA_Q_RANK + MLA_KV_RANK]] + qs_slots
                             + [w_ks, w_vs, w_kr, swap(w_kr)], axis=1).astype(BF16)

    wq = w_q_b.reshape(MLA_Q_RANK, MLA_HEADS, MLA_NOPE + MLA_ROPE)
    q_nope = wq[:, :, :MLA_NOPE].reshape(MLA_Q_RANK, MLA_HEADS * MLA_NOPE)
    q_rope = [jnp.concatenate([wq[:, hd, MLA_NOPE:], swap(wq[:, hd, MLA_NOPE:])], axis=1)
              for hd in range(MLA_HEADS)]
    w_qb_r = jnp.concatenate([q_nope] + q_rope, axis=1).astype(BF16)

    wkv = w_kv_b.reshape(MLA_KV_RANK, MLA_HEADS, MLA_NOPE + MLA_V)
    w_ukT = jnp.transpose(wkv[:, :, :MLA_NOPE], (1, 2, 0)).astype(BF16)
    w_uv = jnp.transpose(wkv[:, :, MLA_NOPE:], (1, 0, 2)).astype(BF16)

    inv = ROPE_THETA ** (-jnp.arange(half, dtype=F32) / half)
    inv_slot = jnp.tile(inv, 2 * MLA_ROPE // half).reshape(1, 2 * MLA_ROPE)
    sgn_slot = jnp.concatenate([jnp.ones((MLA_ROPE,), F32), -jnp.ones((half,), F32),
                                jnp.ones((half,), F32)]).reshape(1, 2 * MLA_ROPE)

    w_ckv = jnp.concatenate([w_ck, w_cv], axis=1).astype(BF16)
    wqT = peer_w_q.T.astype(BF16)
    keys2 = peer_keys.reshape(PEER_HEADS * 2, N_KEYS, PEER_HALF).astype(BF16)
    uT = peer_u.T.astype(BF16)
    v_b = peer_v.astype(BF16)

    k_mem, v_mem = _mem_kv(mem, row(norm_mem), w_ckv)
    kb = _tile(S, 512)
    qcat, kcat, ct_blk, qs, ks, vs = _proj(x2d, pos2d, inv_slot, sgn_slot, row(norm_mix), w_in_r,
                                           row(q_a_norm), w_qb_r, row(kv_a_norm), w_ukT, kb)
    o_lat = _mla(qcat, kcat.reshape(-1, kb, MLA_QK), ct_blk, B, _tile(kb, 256))
    o_b = _swa(swa_sinks, qs, ks, vs, B, S, _tile(S, 512))
    x2, hf, scT = _mid(x2d, o_lat, o_b, k_mem, v_mem, w_uv, row(out_norm_mla), row(out_norm_swa),
                       w_out.astype(BF16), row(norm_cross), w_cq.astype(BF16), w_co.astype(BF16),
                       row(norm_ffn), wqT, keys2, S, _tile(S, 256))
    sel_i, sel_j, sel_g = _topk(scT, _tile(S, 128))
    return hf, sel_i, sel_j, sel_g, uT, v_b, x2


def kernel(x, mem, positions, norm_mix, w_in, q_a_norm, w_q_b, kv_a_norm, w_kv_b, swa_sinks,
           out_norm_mla, out_norm_swa, w_out, norm_cross, norm_mem, w_cq, w_ck, w_cv, w_co,
           norm_ffn, peer_w_q, peer_keys, peer_u, peer_v, norm_final):
    B, S, D = x.shape
    depth = norm_mix.shape[0]
    assert depth == 1, "the final rmsnorm is fused into the last layer's PEER kernel"
    x2d = x.reshape(B * S, D)
    pos2d = positions.reshape(B * S, 1)
    l = 0
    hf, sel_i, sel_j, sel_g, uT, v_b, x2 = _layer(
        x2d, mem, pos2d, B, S, norm_mix[l], w_in[l], q_a_norm[l], w_q_b[l], kv_a_norm[l], w_kv_b[l],
        swa_sinks[l], out_norm_mla[l], out_norm_swa[l], w_out[l], norm_cross[l], norm_mem[l],
        w_cq[l], w_ck[l], w_cv[l], w_co[l], norm_ffn[l], peer_w_q[l], peer_keys[l], peer_u[l],
        peer_v[l])
    out = _peer(hf, sel_i, sel_j, sel_g, uT, v_b, x2, norm_final.reshape(1, D), _tile(S, 512), 8)
    return out.reshape(B, S, D)
```

```python
import functools
import math

import jax
import jax.numpy as jnp
from jax import lax
from jax.experimental import pallas as pl
from jax.experimental.pallas import tpu as pltpu

F32 = jnp.float32
BF16 = jnp.bfloat16
NEG_INF = float("-inf")
LOG2E = math.log2(math.e)

EPS = 1e-6
ROPE_THETA = 10000.0
MLA_HEADS = 4
MLA_NOPE = 128
MLA_ROPE = 64
MLA_V = 128
MLA_Q_RANK = 256
MLA_KV_RANK = 128
MLA_QK = MLA_KV_RANK + MLA_ROPE
SWA_HEADS = 8
SWA_KV_HEADS = 2
SWA_HD = 64
SWA_BLK = 128
X_HEADS = 4
X_HD = 128
PEER_HEADS = 8
N_KEYS = 128
PEER_HALF = 128
PEER_TOPK = 16

LANES = 128
VMEM_LIMIT = 48 * 1024 * 1024
PEER_VMEM_LIMIT = 56 * 1024 * 1024


def _rms(x, g):
    return x * lax.rsqrt(jnp.mean(x * x, axis=-1, keepdims=True) + EPS) * g


def _dot(a, b):
    return jnp.dot(a, b, preferred_element_type=F32)


def _dot_nt(a, b):
    return lax.dot_general(a, b, (((1,), (1,)), ((), ())), preferred_element_type=F32)


def _const_spec(shape):
    zeros = (0,) * len(shape)
    return pl.BlockSpec(shape, lambda *_: zeros)


def _params(*sem):
    return pltpu.CompilerParams(dimension_semantics=sem, vmem_limit_bytes=VMEM_LIMIT)


def _memkv_kernel(mem_ref, g_ref, w_ref, k_ref, v_ref):
    mn = _rms(mem_ref[0], g_ref[...]).astype(BF16)
    kv = _dot(mn, w_ref[...])
    width = k_ref.shape[-1]
    k_ref[0] = kv[:, :width].astype(BF16)
    v_ref[0] = kv[:, width:].astype(BF16)


def _mem_kv(mem, norm_mem, w_ckv):
    B, M, D = mem.shape
    width = w_ckv.shape[1] // 2
    return pl.pallas_call(
        _memkv_kernel,
        grid=(B,),
        in_specs=[pl.BlockSpec((1, M, D), lambda b: (b, 0, 0)),
                  _const_spec((1, D)), _const_spec(w_ckv.shape)],
        out_specs=[pl.BlockSpec((1, M, width), lambda b: (b, 0, 0))] * 2,
        out_shape=[jax.ShapeDtypeStruct((B, M, width), BF16)] * 2,
        compiler_params=_params("arbitrary"),
        name="mem_kv",
    )(mem, norm_mem, w_ckv)


_C_CQ = 0
_C_CKV = _C_CQ + MLA_Q_RANK
_C_QS = _C_CKV + MLA_KV_RANK
_C_KS = _C_QS + SWA_HEADS * LANES
_C_VS = _C_KS + SWA_KV_HEADS * SWA_HD
_C_KR = _C_VS + SWA_KV_HEADS * SWA_HD
_C_END = _C_KR + 2 * MLA_ROPE


def _proj_kernel(x_ref, pos_ref, inv_ref, sgn_ref, nmix_ref, win_ref, qan_ref, wqb_ref, kvan_ref,
                 wuk_ref, qcat_ref, kcat_ref, ct_ref, qs_ref, ks_ref, vs_ref):
    h = _rms(x_ref[...], nmix_ref[...]).astype(BF16)
    proj = _dot(h, win_ref[...])
    qs_ref[...] = proj[:, _C_QS:_C_KS].astype(BF16)
    ks_ref[...] = proj[:, _C_KS:_C_VS].astype(BF16)
    vs_ref[...] = proj[:, _C_VS:_C_KR].astype(BF16)

    ang = pos_ref[...].astype(F32) * inv_ref[...]
    lane = lax.broadcasted_iota(jnp.int32, ang.shape, 1)
    cs = jnp.where(lane < MLA_ROPE, jnp.cos(ang), jnp.sin(ang) * sgn_ref[...])

    def rope_slot(slot):
        r = slot * cs
        return (r + pltpu.roll(r, MLA_ROPE, 1))[:, :MLA_ROPE]

    c = _rms(proj[:, _C_CKV:_C_QS], kvan_ref[...])
    k_r = rope_slot(proj[:, _C_KR:_C_END])
    kcat_ref[...] = jnp.concatenate([c, k_r], axis=1).astype(BF16)
    ct_ref[0] = c.T.astype(BF16)

    qn = _rms(proj[:, _C_CQ:_C_CKV], qan_ref[...]).astype(BF16)
    q2 = _dot(qn, wqb_ref[...])
    scale = float(MLA_NOPE + MLA_ROPE) ** -0.5 * LOG2E
    rope_base = MLA_HEADS * MLA_NOPE
    for hd in range(MLA_HEADS):
        q_lat = _dot(q2[:, hd * MLA_NOPE:(hd + 1) * MLA_NOPE].astype(BF16), wuk_ref[hd])
        q_r = rope_slot(q2[:, rope_base + hd * LANES: rope_base + (hd + 1) * LANES])
        qcat_ref[hd] = (jnp.concatenate([q_lat, q_r], axis=1) * scale).astype(BF16)


def _proj(x2d, pos2d, inv_slot, sgn_slot, norm_mix, w_in_r, q_a_norm, w_qb_r, kv_a_norm, w_ukT, tile):
    T, D = x2d.shape
    qs_w = SWA_HEADS * LANES
    kv_w = SWA_KV_HEADS * SWA_HD
    row = lambda i: (i, 0)
    return pl.pallas_call(
        _proj_kernel,
        grid=(T // tile,),
        in_specs=[pl.BlockSpec((tile, D), row), pl.BlockSpec((tile, 1), row),
                  _const_spec(inv_slot.shape), _const_spec(sgn_slot.shape),
                  _const_spec(norm_mix.shape), _const_spec(w_in_r.shape),
                  _const_spec(q_a_norm.shape), _const_spec(w_qb_r.shape),
                  _const_spec(kv_a_norm.shape), _const_spec(w_ukT.shape)],
        out_specs=[pl.BlockSpec((MLA_HEADS, tile, MLA_QK), lambda i: (0, i, 0)),
                   pl.BlockSpec((tile, MLA_QK), row),
                   pl.BlockSpec((1, MLA_KV_RANK, tile), lambda i: (i, 0, 0)),
                   pl.BlockSpec((tile, qs_w), row),
                   pl.BlockSpec((tile, kv_w), row),
                   pl.BlockSpec((tile, kv_w), row)],
        out_shape=[jax.ShapeDtypeStruct((MLA_HEADS, T, MLA_QK), BF16),
                   jax.ShapeDtypeStruct((T, MLA_QK), BF16),
                   jax.ShapeDtypeStruct((T // tile, MLA_KV_RANK, tile), BF16),
                   jax.ShapeDtypeStruct((T, qs_w), BF16),
                   jax.ShapeDtypeStruct((T, kv_w), BF16),
                   jax.ShapeDtypeStruct((T, kv_w), BF16)],
        compiler_params=_params("arbitrary"),
        name="proj",
    )(x2d, pos2d, inv_slot, sgn_slot, norm_mix, w_in_r, q_a_norm, w_qb_r, kv_a_norm, w_ukT)


def _mla_kernel(q_ref, k_ref, ct_ref, o_ref, sa_ref, sb_ref, m_ref, l_ref, acc_ref, *, tq, kb):
    i = pl.program_id(1)
    rows = MLA_HEADS * tq
    q = q_ref[...].reshape(rows, MLA_QK)
    m_ref[...] = jnp.full(m_ref.shape, NEG_INF, F32)
    l_ref[...] = jnp.zeros(l_ref.shape, F32)
    acc_ref[...] = jnp.zeros(acc_ref.shape, F32)

    def scores(j, s_ref):
        s_ref[...] = _dot_nt(k_ref[j], q)

    def update(j, s_ref, masked):
        s = s_ref[...]
        if masked:
            q_idx = i * tq + (lax.broadcasted_iota(jnp.int32, s.shape, 1) & (tq - 1))
            k_idx = j * kb + lax.broadcasted_iota(jnp.int32, s.shape, 0)
            s = jnp.where(k_idx <= q_idx, s, NEG_INF)
        m_prev = m_ref[...]
        m_new = jnp.maximum(m_prev, jnp.max(s, axis=0, keepdims=True))
        alpha = jnp.exp2(m_prev - m_new)
        p = jnp.exp2(s - m_new)
        l_ref[...] = alpha * l_ref[...] + jnp.sum(p, axis=0, keepdims=True)
        acc_ref[...] = alpha * acc_ref[...] + _dot(ct_ref[j], p.astype(BF16))
        m_ref[...] = m_new

    n_full = (i * tq) // kb
    scores(0, sa_ref)

    def body(t, carry):
        j = 2 * t
        scores(j + 1, sb_ref)
        update(j, sa_ref, False)
        scores(j + 2, sa_ref)
        update(j + 1, sb_ref, False)
        return carry

    lax.fori_loop(0, n_full // 2, body, 0)

    @pl.when(n_full % 2 == 0)
    def _():
        update(n_full, sa_ref, True)

    @pl.when(n_full % 2 == 1)
    def _():
        scores(n_full, sb_ref)
        update(n_full - 1, sa_ref, False)
        update(n_full, sb_ref, True)

    o = acc_ref[...] / l_ref[...]
    for hd in range(MLA_HEADS):
        o_ref[:, hd * MLA_KV_RANK:(hd + 1) * MLA_KV_RANK] = o[:, hd * tq:(hd + 1) * tq].T.astype(BF16)


def _mla(qcat, kcat_blk, ct_blk, B, tq):
    nkb, kb, _ = kcat_blk.shape
    S = nkb * kb // B
    nq = S // tq
    return pl.pallas_call(
        functools.partial(_mla_kernel, tq=tq, kb=kb),
        grid=(B, nq),
        in_specs=[pl.BlockSpec((MLA_HEADS, tq, MLA_QK), lambda b, i: (0, b * nq + i, 0)),
                  pl.BlockSpec((nkb // B, kb, MLA_QK), lambda b, i: (b, 0, 0)),
                  pl.BlockSpec((nkb // B, MLA_KV_RANK, kb), lambda b, i: (b, 0, 0))],
        out_specs=pl.BlockSpec((tq, MLA_HEADS * MLA_KV_RANK), lambda b, i: (b * nq + i, 0)),
        out_shape=jax.ShapeDtypeStruct((B * S, MLA_HEADS * MLA_KV_RANK), BF16),
        scratch_shapes=[pltpu.VMEM((kb, MLA_HEADS * tq), F32), pltpu.VMEM((kb, MLA_HEADS * tq), F32),
                        pltpu.VMEM((1, MLA_HEADS * tq), F32), pltpu.VMEM((1, MLA_HEADS * tq), F32),
                        pltpu.VMEM((MLA_KV_RANK, MLA_HEADS * tq), F32)],
        compiler_params=_params("arbitrary", "arbitrary"),
        name="mla",
    )(qcat, kcat_blk, ct_blk)


def _swa_kernel(sink_ref, q_ref, k_ref, kp_ref, v_ref, vp_ref, o_ref, kf_ref, vf_ref, *, ts):
    i = pl.program_id(1)
    kf_ref[0:SWA_BLK] = kp_ref[...]
    kf_ref[SWA_BLK:] = k_ref[...]
    vf_ref[0:SWA_BLK] = vp_ref[...]
    vf_ref[SWA_BLK:] = v_ref[...]
    a = lax.broadcasted_iota(jnp.int32, (SWA_BLK, SWA_BLK), 0)
    j = lax.broadcasted_iota(jnp.int32, (SWA_BLK, SWA_BLK), 1)
    dist_cur = (a - j).astype(F32)
    dist_prev = dist_cur + float(SWA_BLK)
    valid_cur = j <= a
    valid_prev = j > a
    low_half = j < SWA_HD
    scale = float(SWA_HD) ** -0.5
    group = SWA_HEADS // SWA_KV_HEADS
    for n in range(ts // SWA_BLK):
        r0 = n * SWA_BLK
        k_prev = kf_ref[r0:r0 + SWA_BLK]
        k_cur = kf_ref[r0 + SWA_BLK:r0 + 2 * SWA_BLK]
        v_prev = vf_ref[r0:r0 + SWA_BLK]
        v_cur = vf_ref[r0 + SWA_BLK:r0 + 2 * SWA_BLK]
        prev_pen = jnp.where(jnp.logical_and(i == 0, n == 0), NEG_INF, 0.0).astype(F32)
        outs = []
        for hh in range(SWA_HEADS):
            slope = 2.0 ** (-(8.0 / SWA_HEADS) * (hh + 1))
            qh = q_ref[r0:r0 + SWA_BLK, hh * LANES:(hh + 1) * LANES]
            sp = _dot_nt(qh, k_prev) * scale - slope * dist_prev
            sc = _dot_nt(qh, k_cur) * scale - slope * dist_cur
            sp = jnp.where(valid_prev, sp, NEG_INF) + prev_pen
            sc = jnp.where(valid_cur, sc, NEG_INF)
            sink = sink_ref[hh]
            m = jnp.maximum(jnp.maximum(jnp.max(sp, axis=-1, keepdims=True),
                                        jnp.max(sc, axis=-1, keepdims=True)), sink)
            ep = jnp.exp(sp - m)
            ec = jnp.exp(sc - m)
            den = (jnp.sum(ep, axis=-1, keepdims=True) + jnp.sum(ec, axis=-1, keepdims=True)
                   + jnp.exp(sink - m))
            o = (_dot(ep.astype(BF16), v_prev) + _dot(ec.astype(BF16), v_cur)) / den
            outs.append(o)
        for pair in range(SWA_HEADS // 2):
            kv = (2 * pair) // group
            oe, oo = outs[2 * pair], outs[2 * pair + 1]
            if kv == 0:
                both = jnp.where(low_half, oe, pltpu.roll(oo, SWA_HD, 1))
            else:
                both = jnp.where(low_half, pltpu.roll(oe, SWA_HD, 1), oo)
            o_ref[r0:r0 + SWA_BLK, pair * LANES:(pair + 1) * LANES] = both.astype(BF16)


def _swa(sinks, qs, ks, vs, B, S, ts):
    T = B * S
    nt = S // ts
    kv_w = SWA_KV_HEADS * SWA_HD
    blk_per_tile = ts // SWA_BLK
    cur = lambda b, i: (b * nt + i, 0)
    prev = lambda b, i: (jnp.maximum((b * nt + i) * blk_per_tile - 1, 0), 0)
    return pl.pallas_call(
        functools.partial(_swa_kernel, ts=ts),
        grid=(B, nt),
        in_specs=[pl.BlockSpec(memory_space=pltpu.SMEM),
                  pl.BlockSpec((ts, SWA_HEADS * LANES), cur),
                  pl.BlockSpec((ts, kv_w), cur), pl.BlockSpec((SWA_BLK, kv_w), prev),
                  pl.BlockSpec((ts, kv_w), cur), pl.BlockSpec((SWA_BLK, kv_w), prev)],
        out_specs=pl.BlockSpec((ts, SWA_HEADS * SWA_HD), cur),
        out_shape=jax.ShapeDtypeStruct((T, SWA_HEADS * SWA_HD), BF16),
        scratch_shapes=[pltpu.VMEM((ts + SWA_BLK, kv_w), BF16), pltpu.VMEM((ts + SWA_BLK, kv_w), BF16)],
        compiler_params=_params("arbitrary", "arbitrary"),
        name="swa",
    )(sinks, qs, ks, ks, vs, vs)


def _mid_kernel(x_ref, ol_ref, ob_ref, km_ref, vm_ref, wuv_ref, onm_ref, ons_ref, wout_ref,
                ncross_ref, wcq_ref, wco_ref, nffn_ref, wqT_ref, keys_ref,
                x2_ref, hf_ref, scT_ref):
    ol = ol_ref[...]
    o_a = jnp.concatenate(
        [_dot(ol[:, hd * MLA_KV_RANK:(hd + 1) * MLA_KV_RANK], wuv_ref[hd]) for hd in range(MLA_HEADS)],
        axis=1)
    mix = jnp.concatenate([_rms(o_a, onm_ref[...]), _rms(ob_ref[...].astype(F32), ons_ref[...])],
                          axis=1).astype(BF16)
    x1 = x_ref[...] + _dot(mix, wout_ref[...])

    hc = _rms(x1, ncross_ref[...]).astype(BF16)
    q = _dot(hc, wcq_ref[...]) * (float(X_HD) ** -0.5)
    heads = []
    for hd in range(X_HEADS):
        sl = slice(hd * X_HD, (hd + 1) * X_HD)
        s = _dot_nt(q[:, sl].astype(BF16), km_ref[0, :, sl])
        e = jnp.exp(s - jnp.max(s, axis=-1, keepdims=True))
        p = e / jnp.sum(e, axis=-1, keepdims=True)
        heads.append(_dot(p.astype(BF16), vm_ref[0, :, sl]))
    x2 = x1 + _dot(jnp.concatenate(heads, axis=1).astype(BF16), wco_ref[...])
    x2_ref[...] = x2

    hf = _rms(x2, nffn_ref[...]).astype(BF16)
    hf_ref[...] = hf
    qpT = _dot_nt(wqT_ref[...], hf)
    for hc_i in range(PEER_HEADS * 2):
        sl = slice(hc_i * PEER_HALF, (hc_i + 1) * PEER_HALF)
        scT_ref[sl, :] = _dot(keys_ref[hc_i], qpT[sl].astype(BF16))


def _mid(x2d, o_lat, o_b, k_mem, v_mem, w_uv, onm, ons, w_out, ncross, w_cq, w_co, nffn, wqT, keys2,
         S, tile):
    T, D = x2d.shape
    row = lambda i: (i, 0)
    mem_idx = lambda i: ((i * tile) // S, 0, 0)
    n_sc = PEER_HEADS * 2 * N_KEYS
    consts = [w_uv, onm, ons, w_out, ncross, w_cq, w_co, nffn, wqT, keys2]
    return pl.pallas_call(
        _mid_kernel,
        grid=(T // tile,),
        in_specs=[pl.BlockSpec((tile, D), row),
                  pl.BlockSpec((tile, o_lat.shape[1]), row),
                  pl.BlockSpec((tile, o_b.shape[1]), row),
                  pl.BlockSpec((1,) + k_mem.shape[1:], mem_idx),
                  pl.BlockSpec((1,) + v_mem.shape[1:], mem_idx)]
                 + [_const_spec(c.shape) for c in consts],
        out_specs=[pl.BlockSpec((tile, D), row), pl.BlockSpec((tile, D), row),
                   pl.BlockSpec((n_sc, tile), lambda i: (0, i))],
        out_shape=[jax.ShapeDtypeStruct((T, D), F32), jax.ShapeDtypeStruct((T, D), BF16),
                   jax.ShapeDtypeStruct((n_sc, T), F32)],
        compiler_params=_params("arbitrary"),
        name="mid",
    )(x2d, o_lat, o_b, k_mem, v_mem, *consts)


_CAND_ROWS = [(0, 0), (0, 8)] + [(a, 0) for a in range(1, 8)]
_N_CAND = 8 * (len(_CAND_ROWS) + 1)


def _topk_kernel(sc_ref, i_ref, j_ref, g_ref, is_ref, js_ref, gs_ref):
    tk = sc_ref.shape[1]
    key_iota = lax.broadcasted_iota(jnp.int32, (N_KEYS, tk), 0)
    k_iota = lax.broadcasted_iota(jnp.int32, (PEER_TOPK, tk), 0)
    sub8 = lax.broadcasted_iota(jnp.int32, (8, tk), 0)
    pos = jnp.concatenate([a * PEER_TOPK + b0 + sub8 for a, b0 in _CAND_ROWS]
                          + [(sub8 + 8) * PEER_TOPK], axis=0)

    zero_s = jnp.zeros((PEER_TOPK, tk), F32)
    zero_i = jnp.zeros((PEER_TOPK, tk), jnp.int32)

    def extract(problems, tie_keys, n_tie):
        def body(k, carry):
            sel = k_iota == k
            out = []
            for (vals, s_out, i_out), tkey in zip(carry, tie_keys):
                m = jnp.max(vals, axis=0, keepdims=True)
                c = jnp.where(vals == m, tkey, n_tie)
                first = jnp.min(c, axis=0, keepdims=True)
                out.append((jnp.where(c == first, NEG_INF, vals),
                            jnp.where(sel, m, s_out), jnp.where(sel, first, i_out)))
            return tuple(out)

        init = tuple((v, zero_s, zero_i) for v in problems)
        res = lax.fori_loop(0, PEER_TOPK, body, init, unroll=True)
        return [(s, i) for _, s, i in res]

    def candidates(s0, i0, s1, i1):
        cs = jnp.concatenate([s0[a:a + 1] + s1[b0:b0 + 8] for a, b0 in _CAND_ROWS]
                             + [s0[8:16] + s1[0:1]], axis=0)
        ci = jnp.concatenate([i0[a:a + 1] * N_KEYS + i1[b0:b0 + 8] for a, b0 in _CAND_ROWS]
                             + [i0[8:16] * N_KEYS + i1[0:1]], axis=0)
        return cs, pos * (N_KEYS * N_KEYS) + ci

    def finish(hd, bs, key):
        bi = key & (N_KEYS * N_KEYS - 1)
        e = jnp.exp(bs - jnp.max(bs, axis=0, keepdims=True))
        g = e / jnp.sum(e, axis=0, keepdims=True)
        rows = pl.ds(pl.multiple_of(hd * PEER_TOPK, PEER_TOPK), PEER_TOPK)
        gs_ref[rows, :] = g
        is_ref[rows, :] = (bi >> 7).astype(F32)
        js_ref[rows, :] = (bi & (N_KEYS - 1)).astype(F32)

    def head_pair(t, carry):
        halves = []
        for hd in (2 * t, 2 * t + 1):
            base = pl.multiple_of(hd * (2 * N_KEYS), 2 * N_KEYS)
            halves.append(extract([sc_ref[pl.ds(base, N_KEYS), :], sc_ref[pl.ds(base + N_KEYS, N_KEYS), :]],
                                  [key_iota, key_iota], N_KEYS))
        cands = [candidates(s0, i0, s1, i1) for (s0, i0), (s1, i1) in halves]
        best = extract([c[0] for c in cands], [c[1] for c in cands], 1 << 30)
        for hd, (bs, key) in zip((2 * t, 2 * t + 1), best):
            finish(hd, bs, key)
        return carry

    lax.fori_loop(0, PEER_HEADS // 2, head_pair, 0)
    for c in range(tk // LANES):
        cols = slice(c * LANES, (c + 1) * LANES)
        i_ref[cols, :] = is_ref[:, cols].T
        j_ref[cols, :] = js_ref[:, cols].T
        g_ref[cols, :] = gs_ref[:, cols].T


def _topk(scT, tk):
    n_sc, T = scT.shape
    n_sel = PEER_HEADS * PEER_TOPK
    out = jax.ShapeDtypeStruct((T, n_sel), F32)
    return pl.pallas_call(
        _topk_kernel,
        grid=(T // tk,),
        in_specs=[pl.BlockSpec((n_sc, tk), lambda i: (0, i))],
        out_specs=[pl.BlockSpec((tk, n_sel), lambda i: (i, 0))] * 3,
        out_shape=[out] * 3,
        scratch_shapes=[pltpu.VMEM((n_sel, tk), F32)] * 3,
        compiler_params=_params("arbitrary"),
        name="topk",
    )(scT)


_W_PAD = 8
_I_SPLIT = 2


def _peer_kernel(hf_ref, i_ref, j_ref, g_ref, uT_ref, v_ref, x2_ref, nf_ref, o_ref, w_ref, stash_ref,
                 y_ref, *, tile, n_blk):
    half = pl.program_id(1)
    s = pl.program_id(2)
    pitch = tile + _W_PAD
    n_i = N_KEYS // _I_SPLIT

    @pl.when(jnp.logical_and(half == 0, s == 0))
    def _():
        sub = lax.broadcasted_iota(jnp.int32, (N_KEYS, N_KEYS), 0).astype(F32)

        def tok(t, carry):
            row = pl.ds(t, 1)
            a_t = jnp.where(sub == i_ref[row, :], 0.5 * g_ref[row, :], 0.0).astype(BF16)
            b_t = jnp.where(sub == j_ref[row, :], 1.0, 0.0).astype(BF16)
            w = _dot_nt(a_t, b_t)
            w_ref[pl.ds(t, n_i, stride=pitch), :] = w[:n_i]
            stash_ref[pl.ds(pl.multiple_of(t * n_i, n_i), n_i), :] = w[n_i:].astype(BF16)
            return carry

        lax.fori_loop(0, tile, tok, 0, unroll=64)
        y_ref[...] = jnp.zeros(y_ref.shape, F32)

    @pl.when(jnp.logical_and(half == 1, s == 0))
    def _():
        def tok(t, carry):
            rows = pl.ds(pl.multiple_of(t * n_i, n_i), n_i)
            w_ref[pl.ds(t, n_i, stride=pitch), :] = stash_ref[rows, :].astype(F32)
            return carry

        lax.fori_loop(0, tile, tok, 0, unroll=64)

    hf = hf_ref[...]
    gated = []
    for c in range(n_blk):
        if c % 2 == 0:
            act = _dot(hf, uT_ref[:, c * N_KEYS:(c + 2) * N_KEYS])
        a = act[:, (c % 2) * N_KEYS:(c % 2 + 1) * N_KEYS]
        w = w_ref[pl.ds(pl.multiple_of((s * n_blk + c) * pitch, 8), tile), :]
        gated.append((a * (1.0 + lax.erf(a * (0.5 ** 0.5))) * w).astype(BF16))
    y_ref[...] += _dot(jnp.concatenate(gated, axis=1), v_ref[...])

    @pl.when(jnp.logical_and(half == _I_SPLIT - 1, s == pl.num_programs(2) - 1))
    def _():
        o_ref[...] = _rms(x2_ref[...] + y_ref[...], nf_ref[...])


def _peer(hf, sel_i, sel_j, sel_g, uT, v, x2, norm_final, tile, n_blk):
    T, D = x2.shape
    n_exp = v.shape[0]
    n_sel = sel_i.shape[1]
    ew = n_blk * N_KEYS
    steps = n_exp // ew // _I_SPLIT
    assert steps * n_blk * _I_SPLIT == N_KEYS
    row = lambda t, h, s: (t, 0)
    params = pltpu.CompilerParams(dimension_semantics=("arbitrary",) * 3,
                                  vmem_limit_bytes=PEER_VMEM_LIMIT)
    return pl.pallas_call(
        functools.partial(_peer_kernel, tile=tile, n_blk=n_blk),
        grid=(T // tile, _I_SPLIT, steps),
        in_specs=[pl.BlockSpec((tile, D), row),
                  pl.BlockSpec((tile, n_sel), row), pl.BlockSpec((tile, n_sel), row),
                  pl.BlockSpec((tile, n_sel), row),
                  pl.BlockSpec((D, ew), lambda t, h, s: (0, h * steps + s)),
                  pl.BlockSpec((ew, D), lambda t, h, s: (h * steps + s, 0)),
                  pl.BlockSpec((tile, D), row),
                  _const_spec(norm_final.shape)],
        out_specs=pl.BlockSpec((tile, D), row),
        out_shape=jax.ShapeDtypeStruct((T, D), F32),
        scratch_shapes=[pltpu.VMEM((N_KEYS // _I_SPLIT * (tile + _W_PAD), N_KEYS), F32),
                        pltpu.VMEM((tile * N_KEYS // _I_SPLIT, N_KEYS), BF16),
                        pltpu.VMEM((tile, D), F32)],
        compiler_params=params,
        name="peer",
    )(hf, sel_i, sel_j, sel_g, uT, v, x2, norm_final)


def _tile(n, want):
    t = min(n, want)
    assert n % t == 0, (n, t)
    return t


def _layer(x2d, mem, pos2d, B, S, norm_mix, w_in, q_a_norm, w_q_b, kv_a_norm, w_kv_b, swa_sinks,
           out_norm_mla, out_norm_swa, w_out, norm_cross, norm_mem, w_cq, w_ck, w_cv, w_co,
           norm_ffn, peer_w_q, peer_keys, peer_u, peer_v):
    D = x2d.shape[1]
    row = lambda g: g.reshape(1, -1)

    o = MLA_Q_RANK + MLA_KV_RANK
    w_kr = w_in[:, o:o + MLA_ROPE]
    o += MLA_ROPE
    w_qs = w_in[:, o:o + SWA_HEADS * SWA_HD].reshape(D, SWA_HEADS, SWA_HD)
    o += SWA_HEADS * SWA_HD
    w_ks = w_in[:, o:o + SWA_KV_HEADS * SWA_HD]
    o += SWA_KV_HEADS * SWA_HD
    w_vs = w_in[:, o:o + SWA_KV_HEADS * SWA_HD]
    group = SWA_HEADS // SWA_KV_HEADS
    zeros = jnp.zeros((D, SWA_HD), w_in.dtype)
    qs_slots = [jnp.concatenate([w_qs[:, hh], zeros] if hh // group == 0 else [zeros, w_qs[:, hh]], axis=1)
                for hh in range(SWA_HEADS)]
    half = MLA_ROPE // 2
    swap = lambda w: jnp.concatenate([w[:, half:], w[:, :half]], axis=1)
    w_in_r = jnp.concatenate([w_in[:, :MLA_Q_RANK + MLA_KV_RANK]] + qs_slots
                             + [w_ks, w_vs, w_kr, swap(w_kr)], axis=1).astype(BF16)

    wq = w_q_b.reshape(MLA_Q_RANK, MLA_HEADS, MLA_NOPE + MLA_ROPE)
    q_nope = wq[:, :, :MLA_NOPE].reshape(MLA_Q_RANK, MLA_HEADS * MLA_NOPE)
    q_rope = [jnp.concatenate([wq[:, hd, MLA_NOPE:], swap(wq[:, hd, MLA_NOPE:])], axis=1)
              for hd in range(MLA_HEADS)]
    w_qb_r = jnp.concatenate([q_nope] + q_rope, axis=1).astype(BF16)

    wkv = w_kv_b.reshape(MLA_KV_RANK, MLA_HEADS, MLA_NOPE + MLA_V)
    w_ukT = jnp.transpose(wkv[:, :, :MLA_NOPE], (1, 2, 0)).astype(BF16)
    w_uv = jnp.transpose(wkv[:, :, MLA_NOPE:], (1, 0, 2)).astype(BF16)

    inv = ROPE_THETA ** (-jnp.arange(half, dtype=F32) / half)
    inv_slot = jnp.tile(inv, 2 * MLA_ROPE // half).reshape(1, 2 * MLA_ROPE)
    sgn_slot = jnp.concatenate([jnp.ones((MLA_ROPE,), F32), -jnp.ones((half,), F32),
                                jnp.ones((half,), F32)]).reshape(1, 2 * MLA_ROPE)

    w_ckv = jnp.concatenate([w_ck, w_cv], axis=1).astype(BF16)
    wqT = peer_w_q.T.astype(BF16)
    keys2 = peer_keys.reshape(PEER_HEADS * 2, N_KEYS, PEER_HALF).astype(BF16)
    uT = peer_u.T.astype(BF16)
    v_b = peer_v.astype(BF16)

    k_mem, v_mem = _mem_kv(mem, row(norm_mem), w_ckv)
    kb = _tile(S, 512)
    qcat, kcat, ct_blk, qs, ks, vs = _proj(x2d, pos2d, inv_slot, sgn_slot, row(norm_mix), w_in_r,
                                           row(q_a_norm), w_qb_r, row(kv_a_norm), w_ukT, kb)
    o_lat = _mla(qcat, kcat.reshape(-1, kb, MLA_QK), ct_blk, B, _tile(kb, 256))
    o_b = _swa(swa_sinks, qs, ks, vs, B, S, _tile(S, 512))
    x2, hf, scT = _mid(x2d, o_lat, o_b, k_mem, v_mem, w_uv, row(out_norm_mla), row(out_norm_swa),
                       w_out.astype(BF16), row(norm_cross), w_cq.astype(BF16), w_co.astype(BF16),
                       row(norm_ffn), wqT, keys2, S, _tile(S, 256))
    sel_i, sel_j, sel_g = _topk(scT, _tile(S, 128))
    return hf, sel_i, sel_j, sel_g, uT, v_b, x2


def kernel(x, mem, positions, norm_mix, w_in, q_a_norm, w_q_b, kv_a_norm, w_kv_b, swa_sinks,
           out_norm_mla, out_norm_swa, w_out, norm_cross, norm_mem, w_cq, w_ck, w_cv, w_co,
           norm_ffn, peer_w_q, peer_keys, peer_u, peer_v, norm_final):
    B, S, D = x.shape
    depth = norm_mix.shape[0]
    assert depth == 1, "the final rmsnorm is fused into the last layer's PEER kernel"
    x2d = x.reshape(B * S, D)
    pos2d = positions.reshape(B * S, 1)
    l = 0
    hf, sel_i, sel_j, sel_g, uT, v_b, x2 = _layer(
        x2d, mem, pos2d, B, S, norm_mix[l], w_in[l], q_a_norm[l], w_q_b[l], kv_a_norm[l], w_kv_b[l],
        swa_sinks[l], out_norm_mla[l], out_norm_swa[l], w_out[l], norm_cross[l], norm_mem[l],
        w_cq[l], w_ck[l], w_cv[l], w_co[l], norm_ffn[l], peer_w_q[l], peer_keys[l], peer_u[l],
        peer_v[l])
    out = _peer(hf, sel_i, sel_j, sel_g, uT, v_b, x2, norm_final.reshape(1, D), _tile(S, 512), 8)
    return out.reshape(B, S, D)
```

```python
import functools
import math

import jax
import jax.numpy as jnp
from jax import lax
from jax.experimental import pallas as pl
from jax.experimental.pallas import tpu as pltpu

F32 = jnp.float32
BF16 = jnp.bfloat16
NEG_INF = float("-inf")
LOG2E = math.log2(math.e)

EPS = 1e-6
ROPE_THETA = 10000.0
MLA_HEADS = 4
MLA_NOPE = 128
MLA_ROPE = 64
MLA_V = 128
MLA_Q_RANK = 256
MLA_KV_RANK = 128
MLA_QK = MLA_KV_RANK + MLA_ROPE
MLA_CT_ROWS = MLA_KV_RANK + 16
SWA_HEADS = 8
SWA_KV_HEADS = 2
SWA_HD = 64
SWA_BLK = 128
X_HEADS = 4
X_HD = 128
PEER_HEADS = 8
N_KEYS = 128
PEER_HALF = 128
PEER_TOPK = 16

LANES = 128
VMEM_LIMIT = 48 * 1024 * 1024
PEER_VMEM_LIMIT = 56 * 1024 * 1024


def _rms(x, g):
    return x * lax.rsqrt(jnp.mean(x * x, axis=-1, keepdims=True) + EPS) * g


def _dot(a, b):
    return jnp.dot(a, b, preferred_element_type=F32)


def _dot_nt(a, b):
    return lax.dot_general(a, b, (((1,), (1,)), ((), ())), preferred_element_type=F32)


def _const_spec(shape):
    zeros = (0,) * len(shape)
    return pl.BlockSpec(shape, lambda *_: zeros)


def _params(*sem):
    return pltpu.CompilerParams(dimension_semantics=sem, vmem_limit_bytes=VMEM_LIMIT)


def _memkv_kernel(mem_ref, g_ref, w_ref, k_ref, v_ref):
    mn = _rms(mem_ref[0], g_ref[...]).astype(BF16)
    kv = _dot(mn, w_ref[...])
    width = k_ref.shape[-1]
    k_ref[0] = kv[:, :width].astype(BF16)
    v_ref[0] = kv[:, width:].astype(BF16)


def _mem_kv(mem, norm_mem, w_ckv):
    B, M, D = mem.shape
    width = w_ckv.shape[1] // 2
    return pl.pallas_call(
        _memkv_kernel,
        grid=(B,),
        in_specs=[pl.BlockSpec((1, M, D), lambda b: (b, 0, 0)),
                  _const_spec((1, D)), _const_spec(w_ckv.shape)],
        out_specs=[pl.BlockSpec((1, M, width), lambda b: (b, 0, 0))] * 2,
        out_shape=[jax.ShapeDtypeStruct((B, M, width), BF16)] * 2,
        compiler_params=_params("arbitrary"),
        name="mem_kv",
    )(mem, norm_mem, w_ckv)


_C_CQ = 0
_C_CKV = _C_CQ + MLA_Q_RANK
_C_QS = _C_CKV + MLA_KV_RANK
_C_KS = _C_QS + SWA_HEADS * LANES
_C_VS = _C_KS + SWA_KV_HEADS * SWA_HD
_C_KR = _C_VS + SWA_KV_HEADS * SWA_HD
_C_END = _C_KR + 2 * MLA_ROPE


def _proj_kernel(x_ref, pos_ref, inv_ref, sgn_ref, nmix_ref, win_ref, qan_ref, wqb_ref, kvan_ref,
                 wuk_ref, qcat_ref, kcat_ref, ct_ref, qs_ref, ks_ref, vs_ref):
    h = _rms(x_ref[...], nmix_ref[...]).astype(BF16)
    proj = _dot(h, win_ref[...])
    for hh in range(SWA_HEADS):
        slot = proj[:, _C_QS + hh * LANES:_C_QS + (hh + 1) * LANES]
        qs_ref[hh] = (slot * (float(SWA_HD) ** -0.5)).astype(BF16)
    ks_ref[...] = proj[:, _C_KS:_C_VS].astype(BF16)
    vs_ref[...] = proj[:, _C_VS:_C_KR].astype(BF16)

    ang = pos_ref[...].astype(F32) * inv_ref[...]
    lane = lax.broadcasted_iota(jnp.int32, ang.shape, 1)
    cs = jnp.where(lane < MLA_ROPE, jnp.cos(ang), jnp.sin(ang) * sgn_ref[...])

    def rope_slot(slot):
        r = slot * cs
        return (r + pltpu.roll(r, MLA_ROPE, 1))[:, :MLA_ROPE]

    c = _rms(proj[:, _C_CKV:_C_QS], kvan_ref[...])
    k_r = rope_slot(proj[:, _C_KR:_C_END])
    kcat_ref[...] = jnp.concatenate([c, k_r], axis=1).astype(BF16)
    ct_ref[0, 0:MLA_KV_RANK, :] = c.T.astype(BF16)
    ct_ref[0, MLA_KV_RANK:, :] = jnp.ones((MLA_CT_ROWS - MLA_KV_RANK, c.shape[0]), BF16)

    qn = _rms(proj[:, _C_CQ:_C_CKV], qan_ref[...]).astype(BF16)
    q2 = _dot(qn, wqb_ref[...])
    scale = float(MLA_NOPE + MLA_ROPE) ** -0.5 * LOG2E
    rope_base = MLA_HEADS * MLA_NOPE
    for hd in range(MLA_HEADS):
        q_lat = _dot(q2[:, hd * MLA_NOPE:(hd + 1) * MLA_NOPE].astype(BF16), wuk_ref[hd])
        q_r = rope_slot(q2[:, rope_base + hd * LANES: rope_base + (hd + 1) * LANES])
        qcat_ref[hd] = (jnp.concatenate([q_lat, q_r], axis=1) * scale).astype(BF16)


def _proj(x2d, pos2d, inv_slot, sgn_slot, norm_mix, w_in_r, q_a_norm, w_qb_r, kv_a_norm, w_ukT, tile):
    T, D = x2d.shape
    qs_w = SWA_HEADS * LANES
    kv_w = SWA_KV_HEADS * SWA_HD
    row = lambda i: (i, 0)
    return pl.pallas_call(
        _proj_kernel,
        grid=(T // tile,),
        in_specs=[pl.BlockSpec((tile, D), row), pl.BlockSpec((tile, 1), row),
                  _const_spec(inv_slot.shape), _const_spec(sgn_slot.shape),
                  _const_spec(norm_mix.shape), _const_spec(w_in_r.shape),
                  _const_spec(q_a_norm.shape), _const_spec(w_qb_r.shape),
                  _const_spec(kv_a_norm.shape), _const_spec(w_ukT.shape)],
        out_specs=[pl.BlockSpec((MLA_HEADS, tile, MLA_QK), lambda i: (0, i, 0)),
                   pl.BlockSpec((tile, MLA_QK), row),
                   pl.BlockSpec((1, MLA_CT_ROWS, tile), lambda i: (i, 0, 0)),
                   pl.BlockSpec((SWA_HEADS, tile, LANES), lambda i: (0, i, 0)),
                   pl.BlockSpec((tile, kv_w), row),
                   pl.BlockSpec((tile, kv_w), row)],
        out_shape=[jax.ShapeDtypeStruct((MLA_HEADS, T, MLA_QK), BF16),
                   jax.ShapeDtypeStruct((T, MLA_QK), BF16),
                   jax.ShapeDtypeStruct((T // tile, MLA_CT_ROWS, tile), BF16),
                   jax.ShapeDtypeStruct((SWA_HEADS, T, LANES), BF16),
                   jax.ShapeDtypeStruct((T, kv_w), BF16),
                   jax.ShapeDtypeStruct((T, kv_w), BF16)],
        compiler_params=_params("arbitrary"),
        name="proj",
    )(x2d, pos2d, inv_slot, sgn_slot, norm_mix, w_in_r, q_a_norm, w_qb_r, kv_a_norm, w_ukT)


def _mla_kernel(q_ref, k_ref, ct_ref, o_ref, sa_ref, sb_ref, m_ref, acc_ref, *, tq, kb):
    i = pl.program_id(1)
    rows = MLA_HEADS * tq
    q = q_ref[...].reshape(rows, MLA_QK)
    m_ref[...] = jnp.full(m_ref.shape, NEG_INF, F32)
    acc_ref[...] = jnp.zeros(acc_ref.shape, F32)

    def scores(j, s_ref):
        s_ref[...] = _dot_nt(k_ref[j], q)

    def update(j, s_ref, masked):
        s = s_ref[...]
        if masked:
            q_idx = i * tq + (lax.broadcasted_iota(jnp.int32, s.shape, 1) & (tq - 1))
            k_idx = j * kb + lax.broadcasted_iota(jnp.int32, s.shape, 0)
            s = jnp.where(k_idx <= q_idx, s, NEG_INF)
        m_prev = m_ref[...]
        m_new = jnp.maximum(m_prev, jnp.max(s, axis=0, keepdims=True))
        alpha = jnp.exp2(m_prev - m_new)
        p = jnp.exp2(s - m_new)
        acc_ref[...] = alpha * acc_ref[...] + _dot(ct_ref[j], p.astype(BF16))
        m_ref[...] = m_new

    n_full = (i * tq) // kb
    scores(0, sa_ref)

    def body(t, carry):
        j = 2 * t
        scores(j + 1, sb_ref)
        update(j, sa_ref, False)
        scores(j + 2, sa_ref)
        update(j + 1, sb_ref, False)
        return carry

    lax.fori_loop(0, n_full // 2, body, 0)

    @pl.when(n_full % 2 == 0)
    def _():
        update(n_full, sa_ref, True)

    @pl.when(n_full % 2 == 1)
    def _():
        scores(n_full, sb_ref)
        update(n_full - 1, sa_ref, False)
        update(n_full, sb_ref, True)

    o = acc_ref[0:MLA_KV_RANK, :] / acc_ref[MLA_KV_RANK:MLA_KV_RANK + 1, :]
    for hd in range(MLA_HEADS):
        o_ref[:, hd * MLA_KV_RANK:(hd + 1) * MLA_KV_RANK] = o[:, hd * tq:(hd + 1) * tq].T.astype(BF16)


def _mla(qcat, kcat_blk, ct_blk, B, tq):
    nkb, kb, _ = kcat_blk.shape
    S = nkb * kb // B
    nq = S // tq
    return pl.pallas_call(
        functools.partial(_mla_kernel, tq=tq, kb=kb),
        grid=(B, nq),
        in_specs=[pl.BlockSpec((MLA_HEADS, tq, MLA_QK), lambda b, i: (0, b * nq + i, 0)),
                  pl.BlockSpec((nkb // B, kb, MLA_QK), lambda b, i: (b, 0, 0)),
                  pl.BlockSpec((nkb // B, MLA_CT_ROWS, kb), lambda b, i: (b, 0, 0))],
        out_specs=pl.BlockSpec((tq, MLA_HEADS * MLA_KV_RANK), lambda b, i: (b * nq + i, 0)),
        out_shape=jax.ShapeDtypeStruct((B * S, MLA_HEADS * MLA_KV_RANK), BF16),
        scratch_shapes=[pltpu.VMEM((kb, MLA_HEADS * tq), F32), pltpu.VMEM((kb, MLA_HEADS * tq), F32),
                        pltpu.VMEM((1, MLA_HEADS * tq), F32),
                        pltpu.VMEM((MLA_CT_ROWS, MLA_HEADS * tq), F32)],
        compiler_params=_params("arbitrary", "arbitrary"),
        name="mla",
    )(qcat, kcat_blk, ct_blk)


def _swa_kernel(sink_ref, bp_ref, bc_ref, q_ref, k_ref, kp_ref, v_ref, vp_ref, o_ref, kf_ref, vf_ref,
                *, ts):
    i = pl.program_id(1)
    kv_w = SWA_KV_HEADS * SWA_HD
    kf_ref[0:SWA_BLK] = kp_ref[...]
    kf_ref[SWA_BLK:] = k_ref[...]
    vf_ref[0:SWA_BLK, 0:kv_w] = vp_ref[...]
    vf_ref[SWA_BLK:, 0:kv_w] = v_ref[...]
    vf_ref[:, kv_w:] = jnp.ones((ts + SWA_BLK, LANES), BF16)
    low_half = lax.broadcasted_iota(jnp.int32, (SWA_BLK, LANES), 1) < SWA_HD
    group = SWA_HEADS // SWA_KV_HEADS
    rows = SWA_HEADS * SWA_BLK
    sink = sink_ref[...]
    for n in range(ts // SWA_BLK):
        r0 = n * SWA_BLK
        q = q_ref[:, r0:r0 + SWA_BLK, :].reshape(rows, LANES)
        prev_pen = jnp.where(jnp.logical_and(i == 0, n == 0), NEG_INF, 0.0).astype(F32)
        sp = _dot_nt(q, kf_ref[r0:r0 + SWA_BLK]) + bp_ref[...] + prev_pen
        sc = _dot_nt(q, kf_ref[r0 + SWA_BLK:r0 + 2 * SWA_BLK]) + bc_ref[...]
        m = jnp.maximum(jnp.max(jnp.maximum(sp, sc), axis=-1, keepdims=True), sink)
        ep = jnp.exp(sp - m).astype(BF16)
        ec = jnp.exp(sc - m).astype(BF16)
        o_ext = _dot(ep, vf_ref[r0:r0 + SWA_BLK]) + _dot(ec, vf_ref[r0 + SWA_BLK:r0 + 2 * SWA_BLK])
        o = o_ext[:, :kv_w] / (o_ext[:, kv_w:] + jnp.exp(sink - m))
        for pair in range(SWA_HEADS // 2):
            kv = (2 * pair) // group
            oe = o[(2 * pair) * SWA_BLK:(2 * pair + 1) * SWA_BLK]
            oo = o[(2 * pair + 1) * SWA_BLK:(2 * pair + 2) * SWA_BLK]
            if kv == 0:
                both = jnp.where(low_half, oe, pltpu.roll(oo, SWA_HD, 1))
            else:
                both = jnp.where(low_half, pltpu.roll(oe, SWA_HD, 1), oo)
            o_ref[r0:r0 + SWA_BLK, pair * LANES:(pair + 1) * LANES] = both.astype(BF16)


def _swa_bias():
    r = jnp.arange(SWA_HEADS * SWA_BLK)
    a = (r % SWA_BLK)[:, None]
    slope = 2.0 ** (-(8.0 / SWA_HEADS) * ((r // SWA_BLK) + 1).astype(F32))[:, None]
    j = jnp.arange(SWA_BLK)[None, :]
    dist_cur = (a - j).astype(F32)
    bias_prev = jnp.where(j > a, -slope * (dist_cur + float(SWA_BLK)), NEG_INF)
    bias_cur = jnp.where(j <= a, -slope * dist_cur, NEG_INF)
    return bias_prev.astype(F32), bias_cur.astype(F32)


def _swa(sinks, qs, ks, vs, B, S, ts):
    T = B * S
    nt = S // ts
    kv_w = SWA_KV_HEADS * SWA_HD
    blk_per_tile = ts // SWA_BLK
    rows = SWA_HEADS * SWA_BLK
    bias_prev, bias_cur = _swa_bias()
    sink_rows = jnp.repeat(sinks.astype(F32), SWA_BLK).reshape(rows, 1)
    cur = lambda b, i: (b * nt + i, 0)
    prev = lambda b, i: (jnp.maximum((b * nt + i) * blk_per_tile - 1, 0), 0)
    return pl.pallas_call(
        functools.partial(_swa_kernel, ts=ts),
        grid=(B, nt),
        in_specs=[_const_spec((rows, 1)), _const_spec((rows, SWA_BLK)), _const_spec((rows, SWA_BLK)),
                  pl.BlockSpec((SWA_HEADS, ts, LANES), lambda b, i: (0, b * nt + i, 0)),
                  pl.BlockSpec((ts, kv_w), cur), pl.BlockSpec((SWA_BLK, kv_w), prev),
                  pl.BlockSpec((ts, kv_w), cur), pl.BlockSpec((SWA_BLK, kv_w), prev)],
        out_specs=pl.BlockSpec((ts, SWA_HEADS * SWA_HD), cur),
        out_shape=jax.ShapeDtypeStruct((T, SWA_HEADS * SWA_HD), BF16),
        scratch_shapes=[pltpu.VMEM((ts + SWA_BLK, kv_w), BF16),
                        pltpu.VMEM((ts + SWA_BLK, kv_w + LANES), BF16)],
        compiler_params=_params("arbitrary", "arbitrary"),
        name="swa",
    )(sink_rows, bias_prev, bias_cur, qs, ks, ks, vs, vs)


def _mid_kernel(x_ref, ol_ref, ob_ref, km_ref, vm_ref, wuv_ref, onm_ref, ons_ref, wout_ref,
                ncross_ref, wcq_ref, wco_ref, nffn_ref, wqT_ref, keys_ref,
                x2_ref, hf_ref, scT_ref):
    ol = ol_ref[...]
    o_a = jnp.concatenate(
        [_dot(ol[:, hd * MLA_KV_RANK:(hd + 1) * MLA_KV_RANK], wuv_ref[hd]) for hd in range(MLA_HEADS)],
        axis=1)
    mix = jnp.concatenate([_rms(o_a, onm_ref[...]), _rms(ob_ref[...].astype(F32), ons_ref[...])],
                          axis=1).astype(BF16)
    x1 = x_ref[...] + _dot(mix, wout_ref[...])

    hc = _rms(x1, ncross_ref[...]).astype(BF16)
    q = _dot(hc, wcq_ref[...]) * (float(X_HD) ** -0.5)
    heads = []
    for hd in range(X_HEADS):
        sl = slice(hd * X_HD, (hd + 1) * X_HD)
        s = _dot_nt(q[:, sl].astype(BF16), km_ref[0, :, sl])
        e = jnp.exp(s - jnp.max(s, axis=-1, keepdims=True))
        p = e / jnp.sum(e, axis=-1, keepdims=True)
        heads.append(_dot(p.astype(BF16), vm_ref[0, :, sl]))
    x2 = x1 + _dot(jnp.concatenate(heads, axis=1).astype(BF16), wco_ref[...])
    x2_ref[...] = x2

    hf = _rms(x2, nffn_ref[...]).astype(BF16)
    hf_ref[...] = hf
    qpT = _dot_nt(wqT_ref[...], hf)
    for hc_i in range(PEER_HEADS * 2):
        sl = slice(hc_i * PEER_HALF, (hc_i + 1) * PEER_HALF)
        scT_ref[sl, :] = _dot(keys_ref[hc_i], qpT[sl].astype(BF16))


def _mid(x2d, o_lat, o_b, k_mem, v_mem, w_uv, onm, ons, w_out, ncross, w_cq, w_co, nffn, wqT, keys2,
         S, tile):
    T, D = x2d.shape
    row = lambda i: (i, 0)
    mem_idx = lambda i: ((i * tile) // S, 0, 0)
    n_sc = PEER_HEADS * 2 * N_KEYS
    consts = [w_uv, onm, ons, w_out, ncross, w_cq, w_co, nffn, wqT, keys2]
    return pl.pallas_call(
        _mid_kernel,
        grid=(T // tile,),
        in_specs=[pl.BlockSpec((tile, D), row),
                  pl.BlockSpec((tile, o_lat.shape[1]), row),
                  pl.BlockSpec((tile, o_b.shape[1]), row),
                  pl.BlockSpec((1,) + k_mem.shape[1:], mem_idx),
                  pl.BlockSpec((1,) + v_mem.shape[1:], mem_idx)]
                 + [_const_spec(c.shape) for c in consts],
        out_specs=[pl.BlockSpec((tile, D), row), pl.BlockSpec((tile, D), row),
                   pl.BlockSpec((n_sc, tile), lambda i: (0, i))],
        out_shape=[jax.ShapeDtypeStruct((T, D), F32), jax.ShapeDtypeStruct((T, D), BF16),
                   jax.ShapeDtypeStruct((n_sc, T), F32)],
        compiler_params=_params("arbitrary"),
        name="mid",
    )(x2d, o_lat, o_b, k_mem, v_mem, *consts)


_CAND_ROWS = [(0, 0), (0, 8)] + [(a, 0) for a in range(1, 8)]
_N_CAND = 8 * (len(_CAND_ROWS) + 1)


def _topk_kernel(sc_ref, i_ref, j_ref, g_ref, is_ref, js_ref, gs_ref):
    tk = sc_ref.shape[1]
    key_iota = lax.broadcasted_iota(jnp.int32, (N_KEYS, tk), 0)
    k_iota = lax.broadcasted_iota(jnp.int32, (PEER_TOPK, tk), 0)
    sub8 = lax.broadcasted_iota(jnp.int32, (8, tk), 0)
    pos = jnp.concatenate([a * PEER_TOPK + b0 + sub8 for a, b0 in _CAND_ROWS]
                          + [(sub8 + 8) * PEER_TOPK], axis=0)

    zero_s = jnp.zeros((PEER_TOPK, tk), F32)
    zero_i = jnp.zeros((PEER_TOPK, tk), jnp.int32)

    def extract(problems, tie_keys, n_tie):
        def body(k, carry):
            sel = k_iota == k
            out = []
            for (vals, s_out, i_out), tkey in zip(carry, tie_keys):
                m = jnp.max(vals, axis=0, keepdims=True)
                c = jnp.where(vals == m, tkey, n_tie)
                first = jnp.min(c, axis=0, keepdims=True)
                out.append((jnp.where(c == first, NEG_INF, vals),
                            jnp.where(sel, m, s_out), jnp.where(sel, first, i_out)))
            return tuple(out)

        init = tuple((v, zero_s, zero_i) for v in problems)
        res = lax.fori_loop(0, PEER_TOPK, body, init, unroll=True)
        return [(s, i) for _, s, i in res]

    def candidates(s0, i0, s1, i1):
        cs = jnp.concatenate([s0[a:a + 1] + s1[b0:b0 + 8] for a, b0 in _CAND_ROWS]
                             + [s0[8:16] + s1[0:1]], axis=0)
        ci = jnp.concatenate([i0[a:a + 1] * N_KEYS + i1[b0:b0 + 8] for a, b0 in _CAND_ROWS]
                             + [i0[8:16] * N_KEYS + i1[0:1]], axis=0)
        return cs, pos * (N_KEYS * N_KEYS) + ci

    def finish(hd, bs, key):
        bi = key & (N_KEYS * N_KEYS - 1)
        e = jnp.exp(bs - jnp.max(bs, axis=0, keepdims=True))
        g = e / jnp.sum(e, axis=0, keepdims=True)
        rows = pl.ds(pl.multiple_of(hd * PEER_TOPK, PEER_TOPK), PEER_TOPK)
        gs_ref[rows, :] = g
        is_ref[rows, :] = (bi >> 7).astype(F32)
        js_ref[rows, :] = (bi & (N_KEYS - 1)).astype(F32)

    def head_pair(t, carry):
        halves = []
        for hd in (2 * t, 2 * t + 1):
            base = pl.multiple_of(hd * (2 * N_KEYS), 2 * N_KEYS)
            halves.append(extract([sc_ref[pl.ds(base, N_KEYS), :], sc_ref[pl.ds(base + N_KEYS, N_KEYS), :]],
                                  [key_iota, key_iota], N_KEYS))
        cands = [candidates(s0, i0, s1, i1) for (s0, i0), (s1, i1) in halves]
        best = extract([c[0] for c in cands], [c[1] for c in cands], 1 << 30)
        for hd, (bs, key) in zip((2 * t, 2 * t + 1), best):
            finish(hd, bs, key)
        return carry

    lax.fori_loop(0, PEER_HEADS // 2, head_pair, 0)
    for c in range(tk // LANES):
        cols = slice(c * LANES, (c + 1) * LANES)
        i_ref[cols, :] = is_ref[:, cols].T
        j_ref[cols, :] = js_ref[:, cols].T
        g_ref[cols, :] = gs_ref[:, cols].T


def _topk(scT, tk):
    n_sc, T = scT.shape
    n_sel = PEER_HEADS * PEER_TOPK
    out = jax.ShapeDtypeStruct((T, n_sel), F32)
    return pl.pallas_call(
        _topk_kernel,
        grid=(T // tk,),
        in_specs=[pl.BlockSpec((n_sc, tk), lambda i: (0, i))],
        out_specs=[pl.BlockSpec((tk, n_sel), lambda i: (i, 0))] * 3,
        out_shape=[out] * 3,
        scratch_shapes=[pltpu.VMEM((n_sel, tk), F32)] * 3,
        compiler_params=_params("arbitrary"),
        name="topk",
    )(scT)


_W_PAD = 8
_I_SPLIT = 2


def _peer_kernel(hf_ref, i_ref, j_ref, g_ref, uT_ref, v_ref, x2_ref, nf_ref, o_ref, w_ref, stash_ref,
                 y_ref, *, tile, n_blk):
    half = pl.program_id(1)
    s = pl.program_id(2)
    pitch = tile + _W_PAD
    n_i = N_KEYS // _I_SPLIT

    @pl.when(jnp.logical_and(half == 0, s == 0))
    def _():
        sub = lax.broadcasted_iota(jnp.int32, (N_KEYS, N_KEYS), 0).astype(F32)

        def tok(t, carry):
            row = pl.ds(t, 1)
            a_t = jnp.where(sub == i_ref[row, :], 0.5 * g_ref[row, :], 0.0).astype(BF16)
            b_t = jnp.where(sub == j_ref[row, :], 1.0, 0.0).astype(BF16)
            w = _dot_nt(a_t, b_t)
            w_ref[pl.ds(t, n_i, stride=pitch), :] = w[:n_i]
            stash_ref[pl.ds(pl.multiple_of(t * n_i, n_i), n_i), :] = w[n_i:].astype(BF16)
            return carry

        lax.fori_loop(0, tile, tok, 0, unroll=64)
        y_ref[...] = jnp.zeros(y_ref.shape, F32)

    @pl.when(jnp.logical_and(half == 1, s == 0))
    def _():
        def tok(t, carry):
            rows = pl.ds(pl.multiple_of(t * n_i, n_i), n_i)
            w_ref[pl.ds(t, n_i, stride=pitch), :] = stash_ref[rows, :].astype(F32)
            return carry

        lax.fori_loop(0, tile, tok, 0, unroll=64)

    hf = hf_ref[...]
    gated = []
    for c in range(n_blk):
        if c % 2 == 0:
            act = _dot(hf, uT_ref[:, c * N_KEYS:(c + 2) * N_KEYS])
        a = act[:, (c % 2) * N_KEYS:(c % 2 + 1) * N_KEYS]
        w = w_ref[pl.ds(pl.multiple_of((s * n_blk + c) * pitch, 8), tile), :]
        gated.append((a * (1.0 + lax.erf(a * (0.5 ** 0.5))) * w).astype(BF16))
    y_ref[...] += _dot(jnp.concatenate(gated, axis=1), v_ref[...])

    @pl.when(jnp.logical_and(half == _I_SPLIT - 1, s == pl.num_programs(2) - 1))
    def _():
        o_ref[...] = _rms(x2_ref[...] + y_ref[...], nf_ref[...])


def _peer(hf, sel_i, sel_j, sel_g, uT, v, x2, norm_final, tile, n_blk):
    T, D = x2.shape
    n_exp = v.shape[0]
    n_sel = sel_i.shape[1]
    ew = n_blk * N_KEYS
    steps = n_exp // ew // _I_SPLIT
    assert steps * n_blk * _I_SPLIT == N_KEYS
    row = lambda t, h, s: (t, 0)
    params = pltpu.CompilerParams(dimension_semantics=("arbitrary",) * 3,
                                  vmem_limit_bytes=PEER_VMEM_LIMIT)
    return pl.pallas_call(
        functools.partial(_peer_kernel, tile=tile, n_blk=n_blk),
        grid=(T // tile, _I_SPLIT, steps),
        in_specs=[pl.BlockSpec((tile, D), row),
                  pl.BlockSpec((tile, n_sel), row), pl.BlockSpec((tile, n_sel), row),
                  pl.BlockSpec((tile, n_sel), row),
                  pl.BlockSpec((D, ew), lambda t, h, s: (0, h * steps + s)),
                  pl.BlockSpec((ew, D), lambda t, h, s: (h * steps + s, 0)),
                  pl.BlockSpec((tile, D), row),
                  _const_spec(norm_final.shape)],
        out_specs=pl.BlockSpec((tile, D), row),
        out_shape=jax.ShapeDtypeStruct((T, D), F32),
        scratch_shapes=[pltpu.VMEM((N_KEYS // _I_SPLIT * (tile + _W_PAD), N_KEYS), F32),
                        pltpu.VMEM((tile * N_KEYS // _I_SPLIT, N_KEYS), BF16),
                        pltpu.VMEM((tile, D), F32)],
        compiler_params=params,
        name="peer",
    )(hf, sel_i, sel_j, sel_g, uT, v, x2, norm_final)


def _tile(n, want):
    t = min(n, want)
    assert n % t == 0, (n, t)
    return t


def _layer(x2d, mem, pos2d, B, S, norm_mix, w_in, q_a_norm, w_q_b, kv_a_norm, w_kv_b, swa_sinks,
           out_norm_mla, out_norm_swa, w_out, norm_cross, norm_mem, w_cq, w_ck, w_cv, w_co,
           norm_ffn, peer_w_q, peer_keys, peer_u, peer_v):
    D = x2d.shape[1]
    row = lambda g: g.reshape(1, -1)

    o = MLA_Q_RANK + MLA_KV_RANK
    w_kr = w_in[:, o:o + MLA_ROPE]
    o += MLA_ROPE
    w_qs = w_in[:, o:o + SWA_HEADS * SWA_HD].reshape(D, SWA_HEADS, SWA_HD)
    o += SWA_HEADS * SWA_HD
    w_ks = w_in[:, o:o + SWA_KV_HEADS * SWA_HD]
    o += SWA_KV_HEADS * SWA_HD
    w_vs = w_in[:, o:o + SWA_KV_HEADS * SWA_HD]
    group = SWA_HEADS // SWA_KV_HEADS
    zeros = jnp.zeros((D, SWA_HD), w_in.dtype)
    qs_slots = [jnp.concatenate([w_qs[:, hh], zeros] if hh // group == 0 else [zeros, w_qs[:, hh]], axis=1)
                for hh in range(SWA_HEADS)]
    half = MLA_ROPE // 2
    swap = lambda w: jnp.concatenate([w[:, half:], w[:, :half]], axis=1)
    w_in_r = jnp.concatenate([w_in[:, :MLA_Q_RANK + MLA_KV_RANK]] + qs_slots
                             + [w_ks, w_vs, w_kr, swap(w_kr)], axis=1).astype(BF16)

    wq = w_q_b.reshape(MLA_Q_RANK, MLA_HEADS, MLA_NOPE + MLA_ROPE)
    q_nope = wq[:, :, :MLA_NOPE].reshape(MLA_Q_RANK, MLA_HEADS * MLA_NOPE)
    q_rope = [jnp.concatenate([wq[:, hd, MLA_NOPE:], swap(wq[:, hd, MLA_NOPE:])], axis=1)
              for hd in range(MLA_HEADS)]
    w_qb_r = jnp.concatenate([q_nope] + q_rope, axis=1).astype(BF16)

    wkv = w_kv_b.reshape(MLA_KV_RANK, MLA_HEADS, MLA_NOPE + MLA_V)
    w_ukT = jnp.transpose(wkv[:, :, :MLA_NOPE], (1, 2, 0)).astype(BF16)
    w_uv = jnp.transpose(wkv[:, :, MLA_NOPE:], (1, 0, 2)).astype(BF16)

    inv = ROPE_THETA ** (-jnp.arange(half, dtype=F32) / half)
    inv_slot = jnp.tile(inv, 2 * MLA_ROPE // half).reshape(1, 2 * MLA_ROPE)
    sgn_slot = jnp.concatenate([jnp.ones((MLA_ROPE,), F32), -jnp.ones((half,), F32),
                                jnp.ones((half,), F32)]).reshape(1, 2 * MLA_ROPE)

    w_ckv = jnp.concatenate([w_ck, w_cv], axis=1).astype(BF16)
    wqT = peer_w_q.T.astype(BF16)
    keys2 = peer_keys.reshape(PEER_HEADS * 2, N_KEYS, PEER_HALF).astype(BF16)
    uT = peer_u.T.astype(BF16)
    v_b = peer_v.astype(BF16)

    k_mem, v_mem = _mem_kv(mem, row(norm_mem), w_ckv)
    kb = _tile(S, 512)
    qcat, kcat, ct_blk, qs, ks, vs = _proj(x2d, pos2d, inv_slot, sgn_slot, row(norm_mix), w_in_r,
                                           row(q_a_norm), w_qb_r, row(kv_a_norm), w_ukT, kb)
    o_lat = _mla(qcat, kcat.reshape(-1, kb, MLA_QK), ct_blk, B, _tile(kb, 256))
    o_b = _swa(swa_sinks, qs, ks, vs, B, S, _tile(S, 512))
    x2, hf, scT = _mid(x2d, o_lat, o_b, k_mem, v_mem, w_uv, row(out_norm_mla), row(out_norm_swa),
                       w_out.astype(BF16), row(norm_cross), w_cq.astype(BF16), w_co.astype(BF16),
                       row(norm_ffn), wqT, keys2, S, _tile(S, 256))
    sel_i, sel_j, sel_g = _topk(scT, _tile(S, 128))
    return hf, sel_i, sel_j, sel_g, uT, v_b, x2


def kernel(x, mem, positions, norm_mix, w_in, q_a_norm, w_q_b, kv_a_norm, w_kv_b, swa_sinks,
           out_norm_mla, out_norm_swa, w_out, norm_cross, norm_mem, w_cq, w_ck, w_cv, w_co,
           norm_ffn, peer_w_q, peer_keys, peer_u, peer_v, norm_final):
    B, S, D = x.shape
    depth = norm_mix.shape[0]
    assert depth == 1, "the final rmsnorm is fused into the last layer's PEER kernel"
    x2d = x.reshape(B * S, D)
    pos2d = positions.reshape(B * S, 1)
    l = 0
    hf, sel_i, sel_j, sel_g, uT, v_b, x2 = _layer(
        x2d, mem, pos2d, B, S, norm_mix[l], w_in[l], q_a_norm[l], w_q_b[l], kv_a_norm[l], w_kv_b[l],
        swa_sinks[l], out_norm_mla[l], out_norm_swa[l], w_out[l], norm_cross[l], norm_mem[l],
        w_cq[l], w_ck[l], w_cv[l], w_co[l], norm_ffn[l], peer_w_q[l], peer_keys[l], peer_u[l],
        peer_v[l])
    out = _peer(hf, sel_i, sel_j, sel_g, uT, v_b, x2, norm_final.reshape(1, D), _tile(S, 512), 8)
    return out.reshape(B, S, D)
```

```python
import functools
import math

import jax
import jax.numpy as jnp
from jax import lax
from jax.experimental import pallas as pl
from jax.experimental.pallas import tpu as pltpu

F32 = jnp.float32
BF16 = jnp.bfloat16
NEG_INF = float("-inf")
LOG2E = math.log2(math.e)

EPS = 1e-6
ROPE_THETA = 10000.0
MLA_HEADS = 4
MLA_NOPE = 128
MLA_ROPE = 64
MLA_V = 128
MLA_Q_RANK = 256
MLA_KV_RANK = 128
MLA_QK = MLA_KV_RANK + MLA_ROPE
MLA_CT_ROWS = MLA_KV_RANK + 16
SWA_HEADS = 8
SWA_KV_HEADS = 2
SWA_HD = 64
SWA_BLK = 128
X_HEADS = 4
X_HD = 128
PEER_HEADS = 8
N_KEYS = 128
PEER_HALF = 128
PEER_TOPK = 16

LANES = 128
VMEM_LIMIT = 48 * 1024 * 1024
PEER_VMEM_LIMIT = 56 * 1024 * 1024


def _rms(x, g):
    return x * lax.rsqrt(jnp.mean(x * x, axis=-1, keepdims=True) + EPS) * g


def _dot(a, b):
    return jnp.dot(a, b, preferred_element_type=F32)


def _dot_nt(a, b):
    return lax.dot_general(a, b, (((1,), (1,)), ((), ())), preferred_element_type=F32)


def _const_spec(shape):
    zeros = (0,) * len(shape)
    return pl.BlockSpec(shape, lambda *_: zeros)


def _params(*sem):
    return pltpu.CompilerParams(dimension_semantics=sem, vmem_limit_bytes=VMEM_LIMIT)


def _memkv_kernel(mem_ref, g_ref, w_ref, k_ref, v_ref):
    mn = _rms(mem_ref[0], g_ref[...]).astype(BF16)
    kv = _dot(mn, w_ref[...])
    width = k_ref.shape[-1]
    k_ref[0] = kv[:, :width].astype(BF16)
    v_ref[0] = kv[:, width:].astype(BF16)


def _mem_kv(mem, norm_mem, w_ckv):
    B, M, D = mem.shape
    width = w_ckv.shape[1] // 2
    return pl.pallas_call(
        _memkv_kernel,
        grid=(B,),
        in_specs=[pl.BlockSpec((1, M, D), lambda b: (b, 0, 0)),
                  _const_spec((1, D)), _const_spec(w_ckv.shape)],
        out_specs=[pl.BlockSpec((1, M, width), lambda b: (b, 0, 0))] * 2,
        out_shape=[jax.ShapeDtypeStruct((B, M, width), BF16)] * 2,
        compiler_params=_params("arbitrary"),
        name="mem_kv",
    )(mem, norm_mem, w_ckv)


_C_CQ = 0
_C_CKV = _C_CQ + MLA_Q_RANK
_C_QS = _C_CKV + MLA_KV_RANK
_C_KS = _C_QS + SWA_HEADS * LANES
_C_VS = _C_KS + SWA_KV_HEADS * SWA_HD
_C_KR = _C_VS + SWA_KV_HEADS * SWA_HD
_C_END = _C_KR + 2 * MLA_ROPE


def _proj_kernel(x_ref, pos_ref, inv_ref, sgn_ref, nmix_ref, win_ref, qan_ref, wqb_ref, kvan_ref,
                 wuk_ref, qcat_ref, kcat_ref, ct_ref, qs_ref, ks_ref, vs_ref):
    h = _rms(x_ref[...], nmix_ref[...]).astype(BF16)
    proj = _dot(h, win_ref[...])
    for hh in range(SWA_HEADS):
        slot = proj[:, _C_QS + hh * LANES:_C_QS + (hh + 1) * LANES]
        qs_ref[hh] = (slot * (float(SWA_HD) ** -0.5)).astype(BF16)
    ks_ref[...] = proj[:, _C_KS:_C_VS].astype(BF16)
    vs_ref[...] = proj[:, _C_VS:_C_KR].astype(BF16)

    ang = pos_ref[...].astype(F32) * inv_ref[...]
    lane = lax.broadcasted_iota(jnp.int32, ang.shape, 1)
    cs = jnp.where(lane < MLA_ROPE, jnp.cos(ang), jnp.sin(ang) * sgn_ref[...])

    def rope_slot(slot):
        r = slot * cs
        return (r + pltpu.roll(r, MLA_ROPE, 1))[:, :MLA_ROPE]

    c = _rms(proj[:, _C_CKV:_C_QS], kvan_ref[...])
    k_r = rope_slot(proj[:, _C_KR:_C_END])
    kcat_ref[...] = jnp.concatenate([c, k_r], axis=1).astype(BF16)
    ct_ref[0, 0:MLA_KV_RANK, :] = c.T.astype(BF16)
    ct_ref[0, MLA_KV_RANK:, :] = jnp.ones((MLA_CT_ROWS - MLA_KV_RANK, c.shape[0]), BF16)

    qn = _rms(proj[:, _C_CQ:_C_CKV], qan_ref[...]).astype(BF16)
    q2 = _dot(qn, wqb_ref[...])
    scale = float(MLA_NOPE + MLA_ROPE) ** -0.5 * LOG2E
    rope_base = MLA_HEADS * MLA_NOPE
    for hd in range(MLA_HEADS):
        q_lat = _dot(q2[:, hd * MLA_NOPE:(hd + 1) * MLA_NOPE].astype(BF16), wuk_ref[hd])
        q_r = rope_slot(q2[:, rope_base + hd * LANES: rope_base + (hd + 1) * LANES])
        qcat_ref[hd] = (jnp.concatenate([q_lat, q_r], axis=1) * scale).astype(BF16)


def _proj(x2d, pos2d, inv_slot, sgn_slot, norm_mix, w_in_r, q_a_norm, w_qb_r, kv_a_norm, w_ukT, tile):
    T, D = x2d.shape
    qs_w = SWA_HEADS * LANES
    kv_w = SWA_KV_HEADS * SWA_HD
    row = lambda i: (i, 0)
    return pl.pallas_call(
        _proj_kernel,
        grid=(T // tile,),
        in_specs=[pl.BlockSpec((tile, D), row), pl.BlockSpec((tile, 1), row),
                  _const_spec(inv_slot.shape), _const_spec(sgn_slot.shape),
                  _const_spec(norm_mix.shape), _const_spec(w_in_r.shape),
                  _const_spec(q_a_norm.shape), _const_spec(w_qb_r.shape),
                  _const_spec(kv_a_norm.shape), _const_spec(w_ukT.shape)],
        out_specs=[pl.BlockSpec((MLA_HEADS, tile, MLA_QK), lambda i: (0, i, 0)),
                   pl.BlockSpec((tile, MLA_QK), row),
                   pl.BlockSpec((1, MLA_CT_ROWS, tile), lambda i: (i, 0, 0)),
                   pl.BlockSpec((SWA_HEADS, tile, LANES), lambda i: (0, i, 0)),
                   pl.BlockSpec((tile, kv_w), row),
                   pl.BlockSpec((tile, kv_w), row)],
        out_shape=[jax.ShapeDtypeStruct((MLA_HEADS, T, MLA_QK), BF16),
                   jax.ShapeDtypeStruct((T, MLA_QK), BF16),
                   jax.ShapeDtypeStruct((T // tile, MLA_CT_ROWS, tile), BF16),
                   jax.ShapeDtypeStruct((SWA_HEADS, T, LANES), BF16),
                   jax.ShapeDtypeStruct((T, kv_w), BF16),
                   jax.ShapeDtypeStruct((T, kv_w), BF16)],
        compiler_params=_params("arbitrary"),
        name="proj",
    )(x2d, pos2d, inv_slot, sgn_slot, norm_mix, w_in_r, q_a_norm, w_qb_r, kv_a_norm, w_ukT)


def _mla_kernel(q_ref, k_ref, ct_ref, o_ref, sa_ref, sb_ref, m_ref, acc_ref, *, tq, kb):
    i = pl.program_id(1)
    rows = MLA_HEADS * tq
    q = q_ref[...].reshape(rows, MLA_QK)
    m_ref[...] = jnp.full(m_ref.shape, NEG_INF, F32)
    acc_ref[...] = jnp.zeros(acc_ref.shape, F32)

    def scores(j, s_ref):
        s_ref[...] = _dot_nt(k_ref[j], q)

    def update(j, s_ref, masked):
        s = s_ref[...]
        if masked:
            q_idx = i * tq + (lax.broadcasted_iota(jnp.int32, s.shape, 1) & (tq - 1))
            k_idx = j * kb + lax.broadcasted_iota(jnp.int32, s.shape, 0)
            s = jnp.where(k_idx <= q_idx, s, NEG_INF)
        m_prev = m_ref[...]
        m_new = jnp.maximum(m_prev, jnp.max(s, axis=0, keepdims=True))
        alpha = jnp.exp2(m_prev - m_new)
        p = jnp.exp2(s - m_new)
        acc_ref[...] = alpha * acc_ref[...] + _dot(ct_ref[j], p.astype(BF16))
        m_ref[...] = m_new

    n_full = (i * tq) // kb
    scores(0, sa_ref)

    def body(t, carry):
        j = 2 * t
        scores(j + 1, sb_ref)
        update(j, sa_ref, False)
        scores(j + 2, sa_ref)
        update(j + 1, sb_ref, False)
        return carry

    lax.fori_loop(0, n_full // 2, body, 0)

    @pl.when(n_full % 2 == 0)
    def _():
        update(n_full, sa_ref, True)

    @pl.when(n_full % 2 == 1)
    def _():
        scores(n_full, sb_ref)
        update(n_full - 1, sa_ref, False)
        update(n_full, sb_ref, True)

    o = acc_ref[0:MLA_KV_RANK, :] / acc_ref[MLA_KV_RANK:MLA_KV_RANK + 1, :]
    for hd in range(MLA_HEADS):
        o_ref[:, hd * MLA_KV_RANK:(hd + 1) * MLA_KV_RANK] = o[:, hd * tq:(hd + 1) * tq].T.astype(BF16)


def _mla(qcat, kcat_blk, ct_blk, B, tq):
    nkb, kb, _ = kcat_blk.shape
    S = nkb * kb // B
    nq = S // tq
    return pl.pallas_call(
        functools.partial(_mla_kernel, tq=tq, kb=kb),
        grid=(B, nq),
        in_specs=[pl.BlockSpec((MLA_HEADS, tq, MLA_QK), lambda b, i: (0, b * nq + i, 0)),
                  pl.BlockSpec((nkb // B, kb, MLA_QK), lambda b, i: (b, 0, 0)),
                  pl.BlockSpec((nkb // B, MLA_CT_ROWS, kb), lambda b, i: (b, 0, 0))],
        out_specs=pl.BlockSpec((tq, MLA_HEADS * MLA_KV_RANK), lambda b, i: (b * nq + i, 0)),
        out_shape=jax.ShapeDtypeStruct((B * S, MLA_HEADS * MLA_KV_RANK), BF16),
        scratch_shapes=[pltpu.VMEM((kb, MLA_HEADS * tq), F32), pltpu.VMEM((kb, MLA_HEADS * tq), F32),
                        pltpu.VMEM((1, MLA_HEADS * tq), F32),
                        pltpu.VMEM((MLA_CT_ROWS, MLA_HEADS * tq), F32)],
        compiler_params=_params("arbitrary", "arbitrary"),
        name="mla",
    )(qcat, kcat_blk, ct_blk)


def _swa_kernel(sink_ref, bp_ref, bc_ref, q_ref, k_ref, kp_ref, v_ref, vp_ref, o_ref, kf_ref, vf_ref,
                *, ts):
    i = pl.program_id(1)
    kv_w = SWA_KV_HEADS * SWA_HD
    kf_ref[0:SWA_BLK] = kp_ref[...]
    kf_ref[SWA_BLK:] = k_ref[...]
    vf_ref[0:SWA_BLK, 0:kv_w] = vp_ref[...]
    vf_ref[SWA_BLK:, 0:kv_w] = v_ref[...]
    vf_ref[:, kv_w:] = jnp.ones((ts + SWA_BLK, LANES), BF16)
    low_half = lax.broadcasted_iota(jnp.int32, (SWA_BLK, LANES), 1) < SWA_HD
    group = SWA_HEADS // SWA_KV_HEADS
    rows = SWA_HEADS * SWA_BLK
    sink = sink_ref[...]
    for n in range(ts // SWA_BLK):
        r0 = n * SWA_BLK
        q = q_ref[:, r0:r0 + SWA_BLK, :].reshape(rows, LANES)
        prev_pen = jnp.where(jnp.logical_and(i == 0, n == 0), NEG_INF, 0.0).astype(F32)
        sp = _dot_nt(q, kf_ref[r0:r0 + SWA_BLK]) + bp_ref[...] + prev_pen
        sc = _dot_nt(q, kf_ref[r0 + SWA_BLK:r0 + 2 * SWA_BLK]) + bc_ref[...]
        m = jnp.maximum(jnp.max(jnp.maximum(sp, sc), axis=-1, keepdims=True), sink)
        ep = jnp.exp(sp - m).astype(BF16)
        ec = jnp.exp(sc - m).astype(BF16)
        o_ext = _dot(ep, vf_ref[r0:r0 + SWA_BLK]) + _dot(ec, vf_ref[r0 + SWA_BLK:r0 + 2 * SWA_BLK])
        o = o_ext[:, :kv_w] / (o_ext[:, kv_w:] + jnp.exp(sink - m))
        for pair in range(SWA_HEADS // 2):
            kv = (2 * pair) // group
            oe = o[(2 * pair) * SWA_BLK:(2 * pair + 1) * SWA_BLK]
            oo = o[(2 * pair + 1) * SWA_BLK:(2 * pair + 2) * SWA_BLK]
            if kv == 0:
                both = jnp.where(low_half, oe, pltpu.roll(oo, SWA_HD, 1))
            else:
                both = jnp.where(low_half, pltpu.roll(oe, SWA_HD, 1), oo)
            o_ref[r0:r0 + SWA_BLK, pair * LANES:(pair + 1) * LANES] = both.astype(BF16)


def _swa_bias():
    r = jnp.arange(SWA_HEADS * SWA_BLK)
    a = (r % SWA_BLK)[:, None]
    slope = 2.0 ** (-(8.0 / SWA_HEADS) * ((r // SWA_BLK) + 1).astype(F32))[:, None]
    j = jnp.arange(SWA_BLK)[None, :]
    dist_cur = (a - j).astype(F32)
    bias_prev = jnp.where(j > a, -slope * (dist_cur + float(SWA_BLK)), NEG_INF)
    bias_cur = jnp.where(j <= a, -slope * dist_cur, NEG_INF)
    return bias_prev.astype(F32), bias_cur.astype(F32)


def _swa(sinks, qs, ks, vs, B, S, ts):
    T = B * S
    nt = S // ts
    kv_w = SWA_KV_HEADS * SWA_HD
    blk_per_tile = ts // SWA_BLK
    rows = SWA_HEADS * SWA_BLK
    bias_prev, bias_cur = _swa_bias()
    sink_rows = jnp.repeat(sinks.astype(F32), SWA_BLK).reshape(rows, 1)
    cur = lambda b, i: (b * nt + i, 0)
    prev = lambda b, i: (jnp.maximum((b * nt + i) * blk_per_tile - 1, 0), 0)
    return pl.pallas_call(
        functools.partial(_swa_kernel, ts=ts),
        grid=(B, nt),
        in_specs=[_const_spec((rows, 1)), _const_spec((rows, SWA_BLK)), _const_spec((rows, SWA_BLK)),
                  pl.BlockSpec((SWA_HEADS, ts, LANES), lambda b, i: (0, b * nt + i, 0)),
                  pl.BlockSpec((ts, kv_w), cur), pl.BlockSpec((SWA_BLK, kv_w), prev),
                  pl.BlockSpec((ts, kv_w), cur), pl.BlockSpec((SWA_BLK, kv_w), prev)],
        out_specs=pl.BlockSpec((ts, SWA_HEADS * SWA_HD), cur),
        out_shape=jax.ShapeDtypeStruct((T, SWA_HEADS * SWA_HD), BF16),
        scratch_shapes=[pltpu.VMEM((ts + SWA_BLK, kv_w), BF16),
                        pltpu.VMEM((ts + SWA_BLK, kv_w + LANES), BF16)],
        compiler_params=_params("arbitrary", "arbitrary"),
        name="swa",
    )(sink_rows, bias_prev, bias_cur, qs, ks, ks, vs, vs)


def _mid_kernel(x_ref, ol_ref, ob_ref, km_ref, vm_ref, wuv_ref, onm_ref, ons_ref, wout_ref,
                ncross_ref, wcq_ref, wco_ref, nffn_ref, wqT_ref, keys_ref,
                x2_ref, hf_ref, scT_ref):
    ol = ol_ref[...]
    o_a = jnp.concatenate(
        [_dot(ol[:, hd * MLA_KV_RANK:(hd + 1) * MLA_KV_RANK], wuv_ref[hd]) for hd in range(MLA_HEADS)],
        axis=1)
    mix = jnp.concatenate([_rms(o_a, onm_ref[...]), _rms(ob_ref[...].astype(F32), ons_ref[...])],
                          axis=1).astype(BF16)
    x1 = x_ref[...] + _dot(mix, wout_ref[...])

    hc = _rms(x1, ncross_ref[...]).astype(BF16)
    q = _dot(hc, wcq_ref[...]) * (float(X_HD) ** -0.5)
    heads = []
    for hd in range(X_HEADS):
        sl = slice(hd * X_HD, (hd + 1) * X_HD)
        s = _dot_nt(q[:, sl].astype(BF16), km_ref[0, :, sl])
        e = jnp.exp(s - jnp.max(s, axis=-1, keepdims=True))
        p = e / jnp.sum(e, axis=-1, keepdims=True)
        heads.append(_dot(p.astype(BF16), vm_ref[0, :, sl]))
    x2 = x1 + _dot(jnp.concatenate(heads, axis=1).astype(BF16), wco_ref[...])
    x2_ref[...] = x2

    hf = _rms(x2, nffn_ref[...]).astype(BF16)
    hf_ref[...] = hf
    qpT = _dot_nt(wqT_ref[...], hf)
    for hc_i in range(PEER_HEADS * 2):
        sl = slice(hc_i * PEER_HALF, (hc_i + 1) * PEER_HALF)
        scT_ref[sl, :] = _dot(keys_ref[hc_i], qpT[sl].astype(BF16))


def _mid(x2d, o_lat, o_b, k_mem, v_mem, w_uv, onm, ons, w_out, ncross, w_cq, w_co, nffn, wqT, keys2,
         S, tile):
    T, D = x2d.shape
    row = lambda i: (i, 0)
    mem_idx = lambda i: ((i * tile) // S, 0, 0)
    n_sc = PEER_HEADS * 2 * N_KEYS
    consts = [w_uv, onm, ons, w_out, ncross, w_cq, w_co, nffn, wqT, keys2]
    return pl.pallas_call(
        _mid_kernel,
        grid=(T // tile,),
        in_specs=[pl.BlockSpec((tile, D), row),
                  pl.BlockSpec((tile, o_lat.shape[1]), row),
                  pl.BlockSpec((tile, o_b.shape[1]), row),
                  pl.BlockSpec((1,) + k_mem.shape[1:], mem_idx),
                  pl.BlockSpec((1,) + v_mem.shape[1:], mem_idx)]
                 + [_const_spec(c.shape) for c in consts],
        out_specs=[pl.BlockSpec((tile, D), row), pl.BlockSpec((tile, D), row),
                   pl.BlockSpec((n_sc, tile), lambda i: (0, i))],
        out_shape=[jax.ShapeDtypeStruct((T, D), F32), jax.ShapeDtypeStruct((T, D), BF16),
                   jax.ShapeDtypeStruct((n_sc, T), F32)],
        compiler_params=_params("arbitrary"),
        name="mid",
    )(x2d, o_lat, o_b, k_mem, v_mem, *consts)


_CAND_ROWS = [(0, 0), (0, 8)] + [(a, 0) for a in range(1, 8)]
_N_CAND = 8 * (len(_CAND_ROWS) + 1)


def _topk_head_pair(load, tk, between=(None, None)):
    key_iota = lax.broadcasted_iota(jnp.int32, (N_KEYS, tk), 0)
    k_iota = lax.broadcasted_iota(jnp.int32, (PEER_TOPK, tk), 0)
    sub8 = lax.broadcasted_iota(jnp.int32, (8, tk), 0)
    pos = jnp.concatenate([a * PEER_TOPK + b0 + sub8 for a, b0 in _CAND_ROWS]
                          + [(sub8 + 8) * PEER_TOPK], axis=0)

    zero_s = jnp.zeros((PEER_TOPK, tk), F32)
    zero_i = jnp.zeros((PEER_TOPK, tk), jnp.int32)

    def extract(problems, tie_keys, n_tie):
        def body(k, carry):
            sel = k_iota == k
            out = []
            for (vals, s_out, i_out), tkey in zip(carry, tie_keys):
                m = jnp.max(vals, axis=0, keepdims=True)
                c = jnp.where(vals == m, tkey, n_tie)
                first = jnp.min(c, axis=0, keepdims=True)
                out.append((jnp.where(c == first, NEG_INF, vals),
                            jnp.where(sel, m, s_out), jnp.where(sel, first, i_out)))
            return tuple(out)

        init = tuple((v, zero_s, zero_i) for v in problems)
        res = lax.fori_loop(0, PEER_TOPK, body, init, unroll=True)
        return [(s, i) for _, s, i in res]

    def candidates(s0, i0, s1, i1):
        cs = jnp.concatenate([s0[a:a + 1] + s1[b0:b0 + 8] for a, b0 in _CAND_ROWS]
                             + [s0[8:16] + s1[0:1]], axis=0)
        ci = jnp.concatenate([i0[a:a + 1] * N_KEYS + i1[b0:b0 + 8] for a, b0 in _CAND_ROWS]
                             + [i0[8:16] * N_KEYS + i1[0:1]], axis=0)
        return cs, pos * (N_KEYS * N_KEYS) + ci

    halves = []
    for h in range(2):
        halves.append(extract([load(2 * h), load(2 * h + 1)], [key_iota, key_iota], N_KEYS))
        if between[h] is not None:
            between[h]()
    cands =[candidates(s0, i0, s1, i1) for (s0, i0), (s1, i1) in halves]
    best = extract([c[0] for c in cands], [c[1] for c in cands], 1 << 30)
    out = []
    for bs, key in best:
        bi = key & (N_KEYS * N_KEYS - 1)
        e = jnp.exp(bs - jnp.max(bs, axis=0, keepdims=True))
        g = e / jnp.sum(e, axis=0, keepdims=True)
        out.append((g, (bi >> 7).astype(F32), (bi & (N_KEYS - 1)).astype(F32)))
    return out


_PAIR_ROWS = 2 * 2 * N_KEYS
_PAIR_SEL = 2 * PEER_TOPK


def _topk_kernel(sc_ref, i_ref, j_ref, g_ref, is_ref, js_ref, gs_ref):
    tk = sc_ref.shape[1]

    def head_pair(hp, carry):
        base = pl.multiple_of(hp * _PAIR_ROWS, _PAIR_ROWS)
        res = _topk_head_pair(lambda r: sc_ref[pl.ds(base + r * N_KEYS, N_KEYS), :], tk)
        for h, (g, fi, fj) in enumerate(res):
            rows = pl.ds(pl.multiple_of(hp * _PAIR_SEL + h * PEER_TOPK, PEER_TOPK), PEER_TOPK)
            gs_ref[rows, :] = g
            is_ref[rows, :] = fi
            js_ref[rows, :] = fj
        return carry

    lax.fori_loop(0, PEER_HEADS // 2, head_pair, 0)
    i_ref[...] = is_ref[...].T
    j_ref[...] = js_ref[...].T
    g_ref[...] = gs_ref[...].T


def _topk_first_tile(scT, tile):
    n_sc = scT.shape[0]
    n_sel = PEER_HEADS * PEER_TOPK
    out = jax.ShapeDtypeStruct((tile, n_sel), F32)
    return pl.pallas_call(
        _topk_kernel,
        grid=(tile // LANES,),
        in_specs=[pl.BlockSpec((n_sc, LANES), lambda i: (0, i))],
        out_specs=[pl.BlockSpec((LANES, n_sel), lambda i: (i, 0))] * 3,
        out_shape=[out] * 3,
        scratch_shapes=[pltpu.VMEM((n_sel, LANES), F32)] * 3,
        compiler_params=_params("arbitrary"),
        name="topk",
    )(scT)


_W_PAD = 8
_I_SPLIT = 2


def _peer_kernel(hf_ref, i0_ref, j0_ref, g0_ref, sc_ref, uT_ref, v_ref, x2_ref, nf_ref, o_ref,
                 w_ref, stash_ref, y_ref, i_ref, j_ref, g_ref, in_ref, jn_ref, gn_ref, *, tile, n_blk):
    t_idx = pl.program_id(0)
    half = pl.program_id(1)
    s = pl.program_id(2)
    pitch = tile + _W_PAD
    n_i = N_KEYS // _I_SPLIT
    first_step = jnp.logical_and(half == 0, s == 0)

    @pl.when(jnp.logical_and(first_step, t_idx == 0))
    def _():
        i_ref[...] = i0_ref[...]
        j_ref[...] = j0_ref[...]
        g_ref[...] = g0_ref[...]

    @pl.when(jnp.logical_and(first_step, t_idx > 0))
    def _():
        for lb in range(tile // LANES):
            rows = slice(lb * LANES, (lb + 1) * LANES)
            i_ref[rows, :] = in_ref[lb].T
            j_ref[rows, :] = jn_ref[lb].T
            g_ref[rows, :] = gn_ref[lb].T

    @pl.when(first_step)
    def _():
        sub = lax.broadcasted_iota(jnp.int32, (N_KEYS, N_KEYS), 0).astype(F32)

        def tok(t, carry):
            row = pl.ds(t, 1)
            a_t = jnp.where(sub == i_ref[row, :], 0.5 * g_ref[row, :], 0.0).astype(BF16)
            b_t = jnp.where(sub == j_ref[row, :], 1.0, 0.0).astype(BF16)
            w = _dot_nt(a_t, b_t)
            w_ref[pl.ds(t, n_i, stride=pitch), :] = w[:n_i]
            stash_ref[pl.ds(pl.multiple_of(t * n_i, n_i), n_i), :] = w[n_i:].astype(BF16)
            return carry

        lax.fori_loop(0, tile, tok, 0, unroll=64)
        y_ref[...] = jnp.zeros(y_ref.shape, F32)

    @pl.when(jnp.logical_and(half == 1, s == 0))
    def _():
        def tok(t, carry):
            rows = pl.ds(pl.multiple_of(t * n_i, n_i), n_i)
            w_ref[pl.ds(t, n_i, stride=pitch), :] = stash_ref[rows, :].astype(F32)
            return carry

        lax.fori_loop(0, tile, tok, 0, unroll=64)

    hf = hf_ref[...]
    gated = []

    def expert_blocks(c0, c1):
        for c in range(c0, c1, 2):
            act = _dot(hf, uT_ref[:, c * N_KEYS:(c + 2) * N_KEYS])
            for d in range(2):
                a = act[:, d * N_KEYS:(d + 1) * N_KEYS]
                w = w_ref[pl.ds(pl.multiple_of((s * n_blk + c + d) * pitch, 8), tile), :]
                gated.append((a * (1.0 + lax.erf(a * (0.5 ** 0.5))) * w).astype(BF16))

    u = half * pl.num_programs(2) + s
    pairs = PEER_HEADS // 2
    lb = u // pairs
    res = _topk_head_pair(lambda r: sc_ref[r * N_KEYS:(r + 1) * N_KEYS, :], LANES,
                          between=(lambda: expert_blocks(0, n_blk // 2),
                                   lambda: expert_blocks(n_blk // 2, n_blk)))
    y_ref[...] += _dot(jnp.concatenate(gated, axis=1), v_ref[...])

    for h, (g, fi, fj) in enumerate(res):
        rows = pl.ds(pl.multiple_of((u % pairs) * _PAIR_SEL + h * PEER_TOPK, PEER_TOPK), PEER_TOPK)
        gn_ref[lb, rows, :] = g
        in_ref[lb, rows, :] = fi
        jn_ref[lb, rows, :] = fj

    @pl.when(jnp.logical_and(half == _I_SPLIT - 1, s == pl.num_programs(2) - 1))
    def _():
        o_ref[...] = _rms(x2_ref[...] + y_ref[...], nf_ref[...])


def _peer(hf, sel0_i, sel0_j, sel0_g, scT, uT, v, x2, norm_final, tile, n_blk):
    T, D = x2.shape
    n_exp = v.shape[0]
    n_sel = sel0_i.shape[1]
    ew = n_blk * N_KEYS
    steps = n_exp // ew // _I_SPLIT
    assert steps * n_blk * _I_SPLIT == N_KEYS
    n_tiles = T // tile
    blocks = tile // LANES
    pairs = PEER_HEADS // 2
    assert _I_SPLIT * steps == pairs * blocks, "one retrieval unit of the next tile per grid step"
    row = lambda t, h, s: (t, 0)

    def next_scores(t, h, s):
        u = h * steps + s
        return (u % pairs, jnp.minimum(t + 1, n_tiles - 1) * blocks + u // pairs)

    params = pltpu.CompilerParams(dimension_semantics=("arbitrary",) * 3,
                                  vmem_limit_bytes=PEER_VMEM_LIMIT)
    sel_scratch = [pltpu.VMEM((tile, n_sel), F32)] * 3 + [pltpu.VMEM((blocks, n_sel, LANES), F32)] * 3
    return pl.pallas_call(
        functools.partial(_peer_kernel, tile=tile, n_blk=n_blk),
        grid=(n_tiles, _I_SPLIT, steps),
        in_specs=[pl.BlockSpec((tile, D), row),
                  _const_spec((tile, n_sel)), _const_spec((tile, n_sel)), _const_spec((tile, n_sel)),
                  pl.BlockSpec((_PAIR_ROWS, LANES), next_scores),
                  pl.BlockSpec((D, ew), lambda t, h, s: (0, h * steps + s)),
                  pl.BlockSpec((ew, D), lambda t, h, s: (h * steps + s, 0)),
                  pl.BlockSpec((tile, D), row),
                  _const_spec(norm_final.shape)],
        out_specs=pl.BlockSpec((tile, D), row),
        out_shape=jax.ShapeDtypeStruct((T, D), F32),
        scratch_shapes=[pltpu.VMEM((N_KEYS // _I_SPLIT * (tile + _W_PAD), N_KEYS), F32),
                        pltpu.VMEM((tile * N_KEYS // _I_SPLIT, N_KEYS), BF16),
                        pltpu.VMEM((tile, D), F32)] + sel_scratch,
        compiler_params=params,
        name="peer",
    )(hf, sel0_i, sel0_j, sel0_g, scT, uT, v, x2, norm_final)


def _tile(n, want):
    t = min(n, want)
    assert n % t == 0, (n, t)
    return t


def _layer(x2d, mem, pos2d, B, S, norm_mix, w_in, q_a_norm, w_q_b, kv_a_norm, w_kv_b, swa_sinks,
           out_norm_mla, out_norm_swa, w_out, norm_cross, norm_mem, w_cq, w_ck, w_cv, w_co,
           norm_ffn, peer_w_q, peer_keys, peer_u, peer_v):
    D = x2d.shape[1]
    row = lambda g: g.reshape(1, -1)

    o = MLA_Q_RANK + MLA_KV_RANK
    w_kr = w_in[:, o:o + MLA_ROPE]
    o += MLA_ROPE
    w_qs = w_in[:, o:o + SWA_HEADS * SWA_HD].reshape(D, SWA_HEADS, SWA_HD)
    o += SWA_HEADS * SWA_HD
    w_ks = w_in[:, o:o + SWA_KV_HEADS * SWA_HD]
    o += SWA_KV_HEADS * SWA_HD
    w_vs = w_in[:, o:o + SWA_KV_HEADS * SWA_HD]
    group = SWA_HEADS // SWA_KV_HEADS
    zeros = jnp.zeros((D, SWA_HD), w_in.dtype)
    qs_slots = [jnp.concatenate([w_qs[:, hh], zeros] if hh // group == 0 else [zeros, w_qs[:, hh]], axis=1)
                for hh in range(SWA_HEADS)]
    half = MLA_ROPE // 2
    swap = lambda w: jnp.concatenate([w[:, half:], w[:, :half]], axis=1)
    w_in_r = jnp.concatenate([w_in[:, :MLA_Q_RANK + MLA_KV_RANK]] + qs_slots
                             + [w_ks, w_vs, w_kr, swap(w_kr)], axis=1).astype(BF16)

    wq = w_q_b.reshape(MLA_Q_RANK, MLA_HEADS, MLA_NOPE + MLA_ROPE)
    q_nope = wq[:, :, :MLA_NOPE].reshape(MLA_Q_RANK, MLA_HEADS * MLA_NOPE)
    q_rope = [jnp.concatenate([wq[:, hd, MLA_NOPE:], swap(wq[:, hd, MLA_NOPE:])], axis=1)
              for hd in range(MLA_HEADS)]
    w_qb_r = jnp.concatenate([q_nope] + q_rope, axis=1).astype(BF16)

    wkv = w_kv_b.reshape(MLA_KV_RANK, MLA_HEADS, MLA_NOPE + MLA_V)
    w_ukT = jnp.transpose(wkv[:, :, :MLA_NOPE], (1, 2, 0)).astype(BF16)
    w_uv = jnp.transpose(wkv[:, :, MLA_NOPE:], (1, 0, 2)).astype(BF16)

    inv = ROPE_THETA ** (-jnp.arange(half, dtype=F32) / half)
    inv_slot = jnp.tile(inv, 2 * MLA_ROPE // half).reshape(1, 2 * MLA_ROPE)
    sgn_slot = jnp.concatenate([jnp.ones((MLA_ROPE,), F32), -jnp.ones((half,), F32),
                                jnp.ones((half,), F32)]).reshape(1, 2 * MLA_ROPE)

    w_ckv = jnp.concatenate([w_ck, w_cv], axis=1).astype(BF16)
    wqT = peer_w_q.T.astype(BF16)
    keys2 = peer_keys.reshape(PEER_HEADS * 2, N_KEYS, PEER_HALF).astype(BF16)
    uT = peer_u.T.astype(BF16)
    v_b = peer_v.astype(BF16)

    k_mem, v_mem = _mem_kv(mem, row(norm_mem), w_ckv)
    kb = _tile(S, 512)
    qcat, kcat, ct_blk, qs, ks, vs = _proj(x2d, pos2d, inv_slot, sgn_slot, row(norm_mix), w_in_r,
                                           row(q_a_norm), w_qb_r, row(kv_a_norm), w_ukT, kb)
    o_lat = _mla(qcat, kcat.reshape(-1, kb, MLA_QK), ct_blk, B, _tile(kb, 256))
    o_b = _swa(swa_sinks, qs, ks, vs, B, S, _tile(S, 512))
    x2, hf, scT = _mid(x2d, o_lat, o_b, k_mem, v_mem, w_uv, row(out_norm_mla), row(out_norm_swa),
                       w_out.astype(BF16), row(norm_cross), w_cq.astype(BF16), w_co.astype(BF16),
                       row(norm_ffn), wqT, keys2, S, _tile(S, 256))
    return hf, scT, uT, v_b, x2


def kernel(x, mem, positions, norm_mix, w_in, q_a_norm, w_q_b, kv_a_norm, w_kv_b, swa_sinks,
           out_norm_mla, out_norm_swa, w_out, norm_cross, norm_mem, w_cq, w_ck, w_cv, w_co,
           norm_ffn, peer_w_q, peer_keys, peer_u, peer_v, norm_final):
    B, S, D = x.shape
    depth = norm_mix.shape[0]
    assert depth == 1, "the final rmsnorm is fused into the last layer's PEER kernel"
    x2d = x.reshape(B * S, D)
    pos2d = positions.reshape(B * S, 1)
    l = 0
    hf, scT, uT, v_b, x2 = _layer(
        x2d, mem, pos2d, B, S, norm_mix[l], w_in[l], q_a_norm[l], w_q_b[l], kv_a_norm[l], w_kv_b[l],
        swa_sinks[l], out_norm_mla[l], out_norm_swa[l], w_out[l], norm_cross[l], norm_mem[l],
        w_cq[l], w_ck[l], w_cv[l], w_co[l], norm_ffn[l], peer_w_q[l], peer_keys[l], peer_u[l],
        peer_v[l])
    tile = _tile(S, 512)
    sel0 = _topk_first_tile(scT, tile)
    out = _peer(hf, *sel0, scT, uT, v_b, x2, norm_final.reshape(1, D), tile, 8)
    return out.reshape(B, S, D)
```

```python
import functools
import math

import jax
import jax.numpy as jnp
from jax import lax
from jax.experimental import pallas as pl
from jax.experimental.pallas import tpu as pltpu

F32 = jnp.float32
BF16 = jnp.bfloat16
NEG_INF = float("-inf")
LOG2E = math.log2(math.e)

EPS = 1e-6
ROPE_THETA = 10000.0
MLA_HEADS = 4
MLA_NOPE = 128
MLA_ROPE = 64
MLA_V = 128
MLA_Q_RANK = 256
MLA_KV_RANK = 128
MLA_QK = MLA_KV_RANK + MLA_ROPE
MLA_CT_ROWS = MLA_KV_RANK + 16
SWA_HEADS = 8
SWA_KV_HEADS = 2
SWA_HD = 64
SWA_BLK = 128
X_HEADS = 4
X_HD = 128
PEER_HEADS = 8
N_KEYS = 128
PEER_HALF = 128
PEER_TOPK = 16

LANES = 128
VMEM_LIMIT = 48 * 1024 * 1024
PEER_VMEM_LIMIT = 56 * 1024 * 1024


def _rms(x, g):
    return x * lax.rsqrt(jnp.mean(x * x, axis=-1, keepdims=True) + EPS) * g


def _dot(a, b):
    return jnp.dot(a, b, preferred_element_type=F32)


def _dot_nt(a, b):
    return lax.dot_general(a, b, (((1,), (1,)), ((), ())), preferred_element_type=F32)


def _const_spec(shape):
    zeros = (0,) * len(shape)
    return pl.BlockSpec(shape, lambda *_: zeros)


def _params(*sem):
    return pltpu.CompilerParams(dimension_semantics=sem, vmem_limit_bytes=VMEM_LIMIT)


def _memkv_kernel(mem_ref, g_ref, w_ref, k_ref, v_ref):
    mn = _rms(mem_ref[0], g_ref[...]).astype(BF16)
    kv = _dot(mn, w_ref[...])
    width = k_ref.shape[-1]
    k_ref[0] = kv[:, :width].astype(BF16)
    v_ref[0] = kv[:, width:].astype(BF16)


def _mem_kv(mem, norm_mem, w_ckv):
    B, M, D = mem.shape
    width = w_ckv.shape[1] // 2
    return pl.pallas_call(
        _memkv_kernel,
        grid=(B,),
        in_specs=[pl.BlockSpec((1, M, D), lambda b: (b, 0, 0)),
                  _const_spec((1, D)), _const_spec(w_ckv.shape)],
        out_specs=[pl.BlockSpec((1, M, width), lambda b: (b, 0, 0))] * 2,
        out_shape=[jax.ShapeDtypeStruct((B, M, width), BF16)] * 2,
        compiler_params=_params("arbitrary"),
        name="mem_kv",
    )(mem, norm_mem, w_ckv)


_C_CQ = 0
_C_CKV = _C_CQ + MLA_Q_RANK
_C_QS = _C_CKV + MLA_KV_RANK
_C_KS = _C_QS + SWA_HEADS * LANES
_C_VS = _C_KS + SWA_KV_HEADS * SWA_HD
_C_KR = _C_VS + SWA_KV_HEADS * SWA_HD
_C_END = _C_KR + 2 * MLA_ROPE


def _proj_kernel(x_ref, pos_ref, inv_ref, sgn_ref, nmix_ref, win_ref, qan_ref, wqb_ref, kvan_ref,
                 wuk_ref, qcat_ref, kcat_ref, ct_ref, qs_ref, ks_ref, vs_ref):
    h = _rms(x_ref[...], nmix_ref[...]).astype(BF16)
    proj = _dot(h, win_ref[...])
    for hh in range(SWA_HEADS):
        slot = proj[:, _C_QS + hh * LANES:_C_QS + (hh + 1) * LANES]
        qs_ref[hh] = (slot * (float(SWA_HD) ** -0.5)).astype(BF16)
    ks_ref[...] = proj[:, _C_KS:_C_VS].astype(BF16)
    vs_ref[...] = proj[:, _C_VS:_C_KR].astype(BF16)

    ang = pos_ref[...].astype(F32) * inv_ref[...]
    lane = lax.broadcasted_iota(jnp.int32, ang.shape, 1)
    cs = jnp.where(lane < MLA_ROPE, jnp.cos(ang), jnp.sin(ang) * sgn_ref[...])

    def rope_slot(slot):
        r = slot * cs
        return (r + pltpu.roll(r, MLA_ROPE, 1))[:, :MLA_ROPE]

    c = _rms(proj[:, _C_CKV:_C_QS], kvan_ref[...])
    k_r = rope_slot(proj[:, _C_KR:_C_END])
    kcat_ref[...] = jnp.concatenate([c, k_r], axis=1).astype(BF16)
    ct_ref[0, 0:MLA_KV_RANK, :] = c.T.astype(BF16)
    ct_ref[0, MLA_KV_RANK:, :] = jnp.ones((MLA_CT_ROWS - MLA_KV_RANK, c.shape[0]), BF16)

    qn = _rms(proj[:, _C_CQ:_C_CKV], qan_ref[...]).astype(BF16)
    q2 = _dot(qn, wqb_ref[...])
    scale = float(MLA_NOPE + MLA_ROPE) ** -0.5 * LOG2E
    rope_base = MLA_HEADS * MLA_NOPE
    for hd in range(MLA_HEADS):
        q_lat = _dot(q2[:, hd * MLA_NOPE:(hd + 1) * MLA_NOPE].astype(BF16), wuk_ref[hd])
        q_r = rope_slot(q2[:, rope_base + hd * LANES: rope_base + (hd + 1) * LANES])
        qcat_ref[hd] = (jnp.concatenate([q_lat, q_r], axis=1) * scale).astype(BF16)


def _proj(x2d, pos2d, inv_slot, sgn_slot, norm_mix, w_in_r, q_a_norm, w_qb_r, kv_a_norm, w_ukT, tile):
    T, D = x2d.shape
    qs_w = SWA_HEADS * LANES
    kv_w = SWA_KV_HEADS * SWA_HD
    row = lambda i: (i, 0)
    return pl.pallas_call(
        _proj_kernel,
        grid=(T // tile,),
        in_specs=[pl.BlockSpec((tile, D), row), pl.BlockSpec((tile, 1), row),
                  _const_spec(inv_slot.shape), _const_spec(sgn_slot.shape),
                  _const_spec(norm_mix.shape), _const_spec(w_in_r.shape),
                  _const_spec(q_a_norm.shape), _const_spec(w_qb_r.shape),
                  _const_spec(kv_a_norm.shape), _const_spec(w_ukT.shape)],
        out_specs=[pl.BlockSpec((MLA_HEADS, tile, MLA_QK), lambda i: (0, i, 0)),
                   pl.BlockSpec((tile, MLA_QK), row),
                   pl.BlockSpec((1, MLA_CT_ROWS, tile), lambda i: (i, 0, 0)),
                   pl.BlockSpec((SWA_HEADS, tile, LANES), lambda i: (0, i, 0)),
                   pl.BlockSpec((tile, kv_w), row),
                   pl.BlockSpec((tile, kv_w), row)],
        out_shape=[jax.ShapeDtypeStruct((MLA_HEADS, T, MLA_QK), BF16),
                   jax.ShapeDtypeStruct((T, MLA_QK), BF16),
                   jax.ShapeDtypeStruct((T // tile, MLA_CT_ROWS, tile), BF16),
                   jax.ShapeDtypeStruct((SWA_HEADS, T, LANES), BF16),
                   jax.ShapeDtypeStruct((T, kv_w), BF16),
                   jax.ShapeDtypeStruct((T, kv_w), BF16)],
        compiler_params=_params("arbitrary"),
        name="proj",
    )(x2d, pos2d, inv_slot, sgn_slot, norm_mix, w_in_r, q_a_norm, w_qb_r, kv_a_norm, w_ukT)


def _mla_kernel(q_ref, k_ref, ct_ref, o_ref, sa_ref, sb_ref, m_ref, acc_ref, *, tq, kb):
    i = pl.program_id(1)
    rows = MLA_HEADS * tq
    q = q_ref[...].reshape(rows, MLA_QK)
    m_ref[...] = jnp.full(m_ref.shape, NEG_INF, F32)
    acc_ref[...] = jnp.zeros(acc_ref.shape, F32)

    def scores(j, s_ref):
        s_ref[...] = _dot_nt(k_ref[j], q)

    def update(j, s_ref, masked):
        s = s_ref[...]
        if masked:
            q_idx = i * tq + (lax.broadcasted_iota(jnp.int32, s.shape, 1) & (tq - 1))
            k_idx = j * kb + lax.broadcasted_iota(jnp.int32, s.shape, 0)
            s = jnp.where(k_idx <= q_idx, s, NEG_INF)
        m_prev = m_ref[...]
        m_new = jnp.maximum(m_prev, jnp.max(s, axis=0, keepdims=True))
        alpha = jnp.exp2(m_prev - m_new)
        p = jnp.exp2(s - m_new)
        acc_ref[...] = alpha * acc_ref[...] + _dot(ct_ref[j], p.astype(BF16))
        m_ref[...] = m_new

    n_full = (i * tq) // kb
    scores(0, sa_ref)

    def body(t, carry):
        j = 2 * t
        scores(j + 1, sb_ref)
        update(j, sa_ref, False)
        scores(j + 2, sa_ref)
        update(j + 1, sb_ref, False)
        return carry

    lax.fori_loop(0, n_full // 2, body, 0)

    @pl.when(n_full % 2 == 0)
    def _():
        update(n_full, sa_ref, True)

    @pl.when(n_full % 2 == 1)
    def _():
        scores(n_full, sb_ref)
        update(n_full - 1, sa_ref, False)
        update(n_full, sb_ref, True)

    o = acc_ref[0:MLA_KV_RANK, :] / acc_ref[MLA_KV_RANK:MLA_KV_RANK + 1, :]
    for hd in range(MLA_HEADS):
        o_ref[:, hd * MLA_KV_RANK:(hd + 1) * MLA_KV_RANK] = o[:, hd * tq:(hd + 1) * tq].T.astype(BF16)


def _mla(qcat, kcat_blk, ct_blk, B, tq):
    nkb, kb, _ = kcat_blk.shape
    S = nkb * kb // B
    nq = S // tq
    return pl.pallas_call(
        functools.partial(_mla_kernel, tq=tq, kb=kb),
        grid=(B, nq),
        in_specs=[pl.BlockSpec((MLA_HEADS, tq, MLA_QK), lambda b, i: (0, b * nq + i, 0)),
                  pl.BlockSpec((nkb // B, kb, MLA_QK), lambda b, i: (b, 0, 0)),
                  pl.BlockSpec((nkb // B, MLA_CT_ROWS, kb), lambda b, i: (b, 0, 0))],
        out_specs=pl.BlockSpec((tq, MLA_HEADS * MLA_KV_RANK), lambda b, i: (b * nq + i, 0)),
        out_shape=jax.ShapeDtypeStruct((B * S, MLA_HEADS * MLA_KV_RANK), BF16),
        scratch_shapes=[pltpu.VMEM((kb, MLA_HEADS * tq), F32), pltpu.VMEM((kb, MLA_HEADS * tq), F32),
                        pltpu.VMEM((1, MLA_HEADS * tq), F32),
                        pltpu.VMEM((MLA_CT_ROWS, MLA_HEADS * tq), F32)],
        compiler_params=_params("arbitrary", "arbitrary"),
        name="mla",
    )(qcat, kcat_blk, ct_blk)


def _swa_kernel(sink_ref, bp_ref, bc_ref, q_ref, k_ref, kp_ref, v_ref, vp_ref, o_ref, kf_ref, vf_ref,
                *, ts):
    i = pl.program_id(1)
    kv_w = SWA_KV_HEADS * SWA_HD
    kf_ref[0:SWA_BLK] = kp_ref[...]
    kf_ref[SWA_BLK:] = k_ref[...]
    vf_ref[0:SWA_BLK, 0:kv_w] = vp_ref[...]
    vf_ref[SWA_BLK:, 0:kv_w] = v_ref[...]
    vf_ref[:, kv_w:] = jnp.ones((ts + SWA_BLK, LANES), BF16)
    low_half = lax.broadcasted_iota(jnp.int32, (SWA_BLK, LANES), 1) < SWA_HD
    group = SWA_HEADS // SWA_KV_HEADS
    rows = SWA_HEADS * SWA_BLK
    sink = sink_ref[...]
    for n in range(ts // SWA_BLK):
        r0 = n * SWA_BLK
        q = q_ref[:, r0:r0 + SWA_BLK, :].reshape(rows, LANES)
        prev_pen = jnp.where(jnp.logical_and(i == 0, n == 0), NEG_INF, 0.0).astype(F32)
        sp = _dot_nt(q, kf_ref[r0:r0 + SWA_BLK]) + bp_ref[...] + prev_pen
        sc = _dot_nt(q, kf_ref[r0 + SWA_BLK:r0 + 2 * SWA_BLK]) + bc_ref[...]
        m = jnp.maximum(jnp.max(jnp.maximum(sp, sc), axis=-1, keepdims=True), sink)
        ep = jnp.exp(sp - m).astype(BF16)
        ec = jnp.exp(sc - m).astype(BF16)
        o_ext = _dot(ep, vf_ref[r0:r0 + SWA_BLK]) + _dot(ec, vf_ref[r0 + SWA_BLK:r0 + 2 * SWA_BLK])
        o = o_ext[:, :kv_w] / (o_ext[:, kv_w:] + jnp.exp(sink - m))
        for pair in range(SWA_HEADS // 2):
            kv = (2 * pair) // group
            oe = o[(2 * pair) * SWA_BLK:(2 * pair + 1) * SWA_BLK]
            oo = o[(2 * pair + 1) * SWA_BLK:(2 * pair + 2) * SWA_BLK]
            if kv == 0:
                both = jnp.where(low_half, oe, pltpu.roll(oo, SWA_HD, 1))
            else:
                both = jnp.where(low_half, pltpu.roll(oe, SWA_HD, 1), oo)
            o_ref[r0:r0 + SWA_BLK, pair * LANES:(pair + 1) * LANES] = both.astype(BF16)


def _swa_bias():
    r = jnp.arange(SWA_HEADS * SWA_BLK)
    a = (r % SWA_BLK)[:, None]
    slope = 2.0 ** (-(8.0 / SWA_HEADS) * ((r // SWA_BLK) + 1).astype(F32))[:, None]
    j = jnp.arange(SWA_BLK)[None, :]
    dist_cur = (a - j).astype(F32)
    bias_prev = jnp.where(j > a, -slope * (dist_cur + float(SWA_BLK)), NEG_INF)
    bias_cur = jnp.where(j <= a, -slope * dist_cur, NEG_INF)
    return bias_prev.astype(F32), bias_cur.astype(F32)


def _swa(sinks, qs, ks, vs, B, S, ts):
    T = B * S
    nt = S // ts
    kv_w = SWA_KV_HEADS * SWA_HD
    blk_per_tile = ts // SWA_BLK
    rows = SWA_HEADS * SWA_BLK
    bias_prev, bias_cur = _swa_bias()
    sink_rows = jnp.repeat(sinks.astype(F32), SWA_BLK).reshape(rows, 1)
    cur = lambda b, i: (b * nt + i, 0)
    prev = lambda b, i: (jnp.maximum((b * nt + i) * blk_per_tile - 1, 0), 0)
    return pl.pallas_call(
        functools.partial(_swa_kernel, ts=ts),
        grid=(B, nt),
        in_specs=[_const_spec((rows, 1)), _const_spec((rows, SWA_BLK)), _const_spec((rows, SWA_BLK)),
                  pl.BlockSpec((SWA_HEADS, ts, LANES), lambda b, i: (0, b * nt + i, 0)),
                  pl.BlockSpec((ts, kv_w), cur), pl.BlockSpec((SWA_BLK, kv_w), prev),
                  pl.BlockSpec((ts, kv_w), cur), pl.BlockSpec((SWA_BLK, kv_w), prev)],
        out_specs=pl.BlockSpec((ts, SWA_HEADS * SWA_HD), cur),
        out_shape=jax.ShapeDtypeStruct((T, SWA_HEADS * SWA_HD), BF16),
        scratch_shapes=[pltpu.VMEM((ts + SWA_BLK, kv_w), BF16),
                        pltpu.VMEM((ts + SWA_BLK, kv_w + LANES), BF16)],
        compiler_params=_params("arbitrary", "arbitrary"),
        name="swa",
    )(sink_rows, bias_prev, bias_cur, qs, ks, ks, vs, vs)


def _mid_kernel(x_ref, ol_ref, ob_ref, km_ref, vm_ref, wuv_ref, onm_ref, ons_ref, wout_ref,
                ncross_ref, wcq_ref, wco_ref, nffn_ref, wqT_ref, keys_ref,
                x2_ref, hf_ref, scT_ref):
    ol = ol_ref[...]
    o_a = jnp.concatenate(
        [_dot(ol[:, hd * MLA_KV_RANK:(hd + 1) * MLA_KV_RANK], wuv_ref[hd]) for hd in range(MLA_HEADS)],
        axis=1)
    mix = jnp.concatenate([_rms(o_a, onm_ref[...]), _rms(ob_ref[...].astype(F32), ons_ref[...])],
                          axis=1).astype(BF16)
    x1 = x_ref[...] + _dot(mix, wout_ref[...])

    hc = _rms(x1, ncross_ref[...]).astype(BF16)
    q = _dot(hc, wcq_ref[...]) * (float(X_HD) ** -0.5)
    heads = []
    for hd in range(X_HEADS):
        sl = slice(hd * X_HD, (hd + 1) * X_HD)
        s = _dot_nt(q[:, sl].astype(BF16), km_ref[0, :, sl])
        e = jnp.exp(s - jnp.max(s, axis=-1, keepdims=True))
        p = e / jnp.sum(e, axis=-1, keepdims=True)
        heads.append(_dot(p.astype(BF16), vm_ref[0, :, sl]))
    x2 = x1 + _dot(jnp.concatenate(heads, axis=1).astype(BF16), wco_ref[...])
    x2_ref[...] = x2

    hf = _rms(x2, nffn_ref[...]).astype(BF16)
    hf_ref[...] = hf
    qpT = _dot_nt(wqT_ref[...], hf)
    for hc_i in range(PEER_HEADS * 2):
        sl = slice(hc_i * PEER_HALF, (hc_i + 1) * PEER_HALF)
        scT_ref[sl, :] = _dot(keys_ref[hc_i], qpT[sl].astype(BF16))


def _mid(x2d, o_lat, o_b, k_mem, v_mem, w_uv, onm, ons, w_out, ncross, w_cq, w_co, nffn, wqT, keys2,
         S, tile):
    T, D = x2d.shape
    row = lambda i: (i, 0)
    mem_idx = lambda i: ((i * tile) // S, 0, 0)
    n_sc = PEER_HEADS * 2 * N_KEYS
    consts = [w_uv, onm, ons, w_out, ncross, w_cq, w_co, nffn, wqT, keys2]
    return pl.pallas_call(
        _mid_kernel,
        grid=(T // tile,),
        in_specs=[pl.BlockSpec((tile, D), row),
                  pl.BlockSpec((tile, o_lat.shape[1]), row),
                  pl.BlockSpec((tile, o_b.shape[1]), row),
                  pl.BlockSpec((1,) + k_mem.shape[1:], mem_idx),
                  pl.BlockSpec((1,) + v_mem.shape[1:], mem_idx)]
                 + [_const_spec(c.shape) for c in consts],
        out_specs=[pl.BlockSpec((tile, D), row), pl.BlockSpec((tile, D), row),
                   pl.BlockSpec((n_sc, tile), lambda i: (0, i))],
        out_shape=[jax.ShapeDtypeStruct((T, D), F32), jax.ShapeDtypeStruct((T, D), BF16),
                   jax.ShapeDtypeStruct((n_sc, T), F32)],
        compiler_params=_params("arbitrary"),
        name="mid",
    )(x2d, o_lat, o_b, k_mem, v_mem, *consts)


_CAND_ROWS = [(0, 0), (0, 8)] + [(a, 0) for a in range(1, 8)]
_N_CAND = 8 * (len(_CAND_ROWS) + 1)


def _topk_head_pair(load, tk, between=(None, None), lockstep=True):
    key_iota = lax.broadcasted_iota(jnp.int32, (N_KEYS, tk), 0)
    k_iota = lax.broadcasted_iota(jnp.int32, (PEER_TOPK, tk), 0)
    sub8 = lax.broadcasted_iota(jnp.int32, (8, tk), 0)
    pos = jnp.concatenate([a * PEER_TOPK + b0 + sub8 for a, b0 in _CAND_ROWS]
                          + [(sub8 + 8) * PEER_TOPK], axis=0)

    zero_s = jnp.zeros((PEER_TOPK, tk), F32)
    zero_i = jnp.zeros((PEER_TOPK, tk), jnp.int32)

    def extract(problems, tie_keys, n_tie):
        def body(k, carry):
            sel = k_iota == k
            out = []
            for (vals, s_out, i_out), tkey in zip(carry, tie_keys):
                m = jnp.max(vals, axis=0, keepdims=True)
                c = jnp.where(vals == m, tkey, n_tie)
                first = jnp.min(c, axis=0, keepdims=True)
                out.append((jnp.where(c == first, NEG_INF, vals),
                            jnp.where(sel, m, s_out), jnp.where(sel, first, i_out)))
            return tuple(out)

        init = tuple((v, zero_s, zero_i) for v in problems)
        res = lax.fori_loop(0, PEER_TOPK, body, init, unroll=True)
        return [(s, i) for _, s, i in res]

    def candidates(s0, i0, s1, i1):
        cs = jnp.concatenate([s0[a:a + 1] + s1[b0:b0 + 8] for a, b0 in _CAND_ROWS]
                             + [s0[8:16] + s1[0:1]], axis=0)
        ci = jnp.concatenate([i0[a:a + 1] * N_KEYS + i1[b0:b0 + 8] for a, b0 in _CAND_ROWS]
                             + [i0[8:16] * N_KEYS + i1[0:1]], axis=0)
        return cs, pos * (N_KEYS * N_KEYS) + ci

    def extract_all(problems, tie_keys, n_tie):
        if lockstep:
            return extract(problems, tie_keys, n_tie)
        return [extract([p], [t], n_tie)[0] for p, t in zip(problems, tie_keys)]

    halves = []
    for h in range(2):
        halves.append(extract_all([load(2 * h), load(2 * h + 1)], [key_iota, key_iota], N_KEYS))
        if between[h] is not None:
            between[h]()
    cands = [candidates(s0, i0, s1, i1) for (s0, i0), (s1, i1) in halves]
    best = extract_all([c[0] for c in cands], [c[1] for c in cands], 1 << 30)
    out = []
    for bs, key in best:
        bi = key & (N_KEYS * N_KEYS - 1)
        e = jnp.exp(bs - jnp.max(bs, axis=0, keepdims=True))
        g = e / jnp.sum(e, axis=0, keepdims=True)
        out.append((g, (bi >> 7).astype(F32), (bi & (N_KEYS - 1)).astype(F32)))
    return out


_PAIR_ROWS = 2 * 2 * N_KEYS
_PAIR_SEL = 2 * PEER_TOPK


def _topk_kernel(sc_ref, i_ref, j_ref, g_ref, is_ref, js_ref, gs_ref):
    tk = sc_ref.shape[1]

    def head_pair(hp, carry):
        base = pl.multiple_of(hp * _PAIR_ROWS, _PAIR_ROWS)
        res = _topk_head_pair(lambda r: sc_ref[pl.ds(base + r * N_KEYS, N_KEYS), :], tk)
        for h, (g, fi, fj) in enumerate(res):
            rows = pl.ds(pl.multiple_of(hp * _PAIR_SEL + h * PEER_TOPK, PEER_TOPK), PEER_TOPK)
            gs_ref[rows, :] = g
            is_ref[rows, :] = fi
            js_ref[rows, :] = fj
        return carry

    lax.fori_loop(0, PEER_HEADS // 2, head_pair, 0)
    i_ref[...] = is_ref[...].T
    j_ref[...] = js_ref[...].T
    g_ref[...] = gs_ref[...].T


def _topk_first_tile(scT, tile):
    n_sc = scT.shape[0]
    n_sel = PEER_HEADS * PEER_TOPK
    out = jax.ShapeDtypeStruct((tile, n_sel), F32)
    return pl.pallas_call(
        _topk_kernel,
        grid=(tile // LANES,),
        in_specs=[pl.BlockSpec((n_sc, LANES), lambda i: (0, i))],
        out_specs=[pl.BlockSpec((LANES, n_sel), lambda i: (i, 0))] * 3,
        out_shape=[out] * 3,
        scratch_shapes=[pltpu.VMEM((n_sel, LANES), F32)] * 3,
        compiler_params=_params("arbitrary"),
        name="topk",
    )(scT)


_W_PAD = 8
_I_SPLIT = 2


def _peer_kernel(hf_ref, i0_ref, j0_ref, g0_ref, sc_ref, uT_ref, v_ref, x2_ref, nf_ref, o_ref,
                 w_ref, stash_ref, y_ref, i_ref, j_ref, g_ref, in_ref, jn_ref, gn_ref, *, tile, n_blk):
    t_idx = pl.program_id(0)
    half = pl.program_id(1)
    s = pl.program_id(2)
    pitch = tile + _W_PAD
    n_i = N_KEYS // _I_SPLIT
    first_step = jnp.logical_and(half == 0, s == 0)

    @pl.when(jnp.logical_and(first_step, t_idx == 0))
    def _():
        i_ref[...] = i0_ref[...]
        j_ref[...] = j0_ref[...]
        g_ref[...] = g0_ref[...]

    @pl.when(jnp.logical_and(first_step, t_idx > 0))
    def _():
        for lb in range(tile // LANES):
            rows = slice(lb * LANES, (lb + 1) * LANES)
            i_ref[rows, :] = in_ref[lb].T
            j_ref[rows, :] = jn_ref[lb].T
            g_ref[rows, :] = gn_ref[lb].T

    @pl.when(first_step)
    def _():
        sub = lax.broadcasted_iota(jnp.int32, (N_KEYS, N_KEYS), 0).astype(F32)

        def tok(t, carry):
            row = pl.ds(t, 1)
            a_t = jnp.where(sub == i_ref[row, :], 0.5 * g_ref[row, :], 0.0).astype(BF16)
            b_t = jnp.where(sub == j_ref[row, :], 1.0, 0.0).astype(BF16)
            w = _dot_nt(a_t, b_t)
            w_ref[pl.ds(t, n_i, stride=pitch), :] = w[:n_i]
            stash_ref[pl.ds(pl.multiple_of(t * n_i, n_i), n_i), :] = w[n_i:].astype(BF16)
            return carry

        lax.fori_loop(0, tile, tok, 0, unroll=64)
        y_ref[...] = jnp.zeros(y_ref.shape, F32)

    @pl.when(jnp.logical_and(half == 1, s == 0))
    def _():
        def tok(t, carry):
            rows = pl.ds(pl.multiple_of(t * n_i, n_i), n_i)
            w_ref[pl.ds(t, n_i, stride=pitch), :] = stash_ref[rows, :].astype(F32)
            return carry

        lax.fori_loop(0, tile, tok, 0, unroll=64)

    hf = hf_ref[...]
    gated = []

    def expert_blocks(c0, c1):
        for c in range(c0, c1, 2):
            act = _dot(hf, uT_ref[:, c * N_KEYS:(c + 2) * N_KEYS])
            for d in range(2):
                a = act[:, d * N_KEYS:(d + 1) * N_KEYS]
                w = w_ref[pl.ds(pl.multiple_of((s * n_blk + c + d) * pitch, 8), tile), :]
                gated.append((a * (1.0 + lax.erf(a * (0.5 ** 0.5))) * w).astype(BF16))

    u = half * pl.num_programs(2) + s
    pairs = PEER_HEADS // 2
    lb = u // pairs
    res = _topk_head_pair(lambda r: sc_ref[r * N_KEYS:(r + 1) * N_KEYS, :], LANES,
                          between=(lambda: expert_blocks(0, n_blk // 2),
                                   lambda: expert_blocks(n_blk // 2, n_blk)), lockstep=False)
    y_ref[...] += _dot(jnp.concatenate(gated, axis=1), v_ref[...])

    for h, (g, fi, fj) in enumerate(res):
        rows = pl.ds(pl.multiple_of((u % pairs) * _PAIR_SEL + h * PEER_TOPK, PEER_TOPK), PEER_TOPK)
        gn_ref[lb, rows, :] = g
        in_ref[lb, rows, :] = fi
        jn_ref[lb, rows, :] = fj

    @pl.when(jnp.logical_and(half == _I_SPLIT - 1, s == pl.num_programs(2) - 1))
    def _():
        o_ref[...] = _rms(x2_ref[...] + y_ref[...], nf_ref[...])


def _peer(hf, sel0_i, sel0_j, sel0_g, scT, uT, v, x2, norm_final, tile, n_blk):
    T, D = x2.shape
    n_exp = v.shape[0]
    n_sel = sel0_i.shape[1]
    ew = n_blk * N_KEYS
    steps = n_exp // ew // _I_SPLIT
    assert steps * n_blk * _I_SPLIT == N_KEYS
    n_tiles = T // tile
    blocks = tile // LANES
    pairs = PEER_HEADS // 2
    assert _I_SPLIT * steps == pairs * blocks, "one retrieval unit of the next tile per grid step"
    row = lambda t, h, s: (t, 0)

    def next_scores(t, h, s):
        u = h * steps + s
        return (u % pairs, jnp.minimum(t + 1, n_tiles - 1) * blocks + u // pairs)

    params = pltpu.CompilerParams(dimension_semantics=("arbitrary",) * 3,
                                  vmem_limit_bytes=PEER_VMEM_LIMIT)
    sel_scratch = [pltpu.VMEM((tile, n_sel), F32)] * 3 + [pltpu.VMEM((blocks, n_sel, LANES), F32)] * 3
    return pl.pallas_call(
        functools.partial(_peer_kernel, tile=tile, n_blk=n_blk),
        grid=(n_tiles, _I_SPLIT, steps),
        in_specs=[pl.BlockSpec((tile, D), row),
                  _const_spec((tile, n_sel)), _const_spec((tile, n_sel)), _const_spec((tile, n_sel)),
                  pl.BlockSpec((_PAIR_ROWS, LANES), next_scores),
                  pl.BlockSpec((D, ew), lambda t, h, s: (0, h * steps + s)),
                  pl.BlockSpec((ew, D), lambda t, h, s: (h * steps + s, 0)),
                  pl.BlockSpec((tile, D), row),
                  _const_spec(norm_final.shape)],
        out_specs=pl.BlockSpec((tile, D), row),
        out_shape=jax.ShapeDtypeStruct((T, D), F32),
        scratch_shapes=[pltpu.VMEM((N_KEYS // _I_SPLIT * (tile + _W_PAD), N_KEYS), F32),
                        pltpu.VMEM((tile * N_KEYS // _I_SPLIT, N_KEYS), BF16),
                        pltpu.VMEM((tile, D), F32)] + sel_scratch,
        compiler_params=params,
        name="peer",
    )(hf, sel0_i, sel0_j, sel0_g, scT, uT, v, x2, norm_final)


def _tile(n, want):
    t = min(n, want)
    assert n % t == 0, (n, t)
    return t


def _layer(x2d, mem, pos2d, B, S, norm_mix, w_in, q_a_norm, w_q_b, kv_a_norm, w_kv_b, swa_sinks,
           out_norm_mla, out_norm_swa, w_out, norm_cross, norm_mem, w_cq, w_ck, w_cv, w_co,
           norm_ffn, peer_w_q, peer_keys, peer_u, peer_v):
    D = x2d.shape[1]
    row = lambda g: g.reshape(1, -1)

    o = MLA_Q_RANK + MLA_KV_RANK
    w_kr = w_in[:, o:o + MLA_ROPE]
    o += MLA_ROPE
    w_qs = w_in[:, o:o + SWA_HEADS * SWA_HD].reshape(D, SWA_HEADS, SWA_HD)
    o += SWA_HEADS * SWA_HD
    w_ks = w_in[:, o:o + SWA_KV_HEADS * SWA_HD]
    o += SWA_KV_HEADS * SWA_HD
    w_vs = w_in[:, o:o + SWA_KV_HEADS * SWA_HD]
    group = SWA_HEADS // SWA_KV_HEADS
    zeros = jnp.zeros((D, SWA_HD), w_in.dtype)
    qs_slots = [jnp.concatenate([w_qs[:, hh], zeros] if hh // group == 0 else [zeros, w_qs[:, hh]], axis=1)
                for hh in range(SWA_HEADS)]
    half = MLA_ROPE // 2
    swap = lambda w: jnp.concatenate([w[:, half:], w[:, :half]], axis=1)
    w_in_r = jnp.concatenate([w_in[:, :MLA_Q_RANK + MLA_KV_RANK]] + qs_slots
                             + [w_ks, w_vs, w_kr, swap(w_kr)], axis=1).astype(BF16)

    wq = w_q_b.reshape(MLA_Q_RANK, MLA_HEADS, MLA_NOPE + MLA_ROPE)
    q_nope = wq[:, :, :MLA_NOPE].reshape(MLA_Q_RANK, MLA_HEADS * MLA_NOPE)
    q_rope = [jnp.concatenate([wq[:, hd, MLA_NOPE:], swap(wq[:, hd, MLA_NOPE:])], axis=1)
              for hd in range(MLA_HEADS)]
    w_qb_r = jnp.concatenate([q_nope] + q_rope, axis=1).astype(BF16)

    wkv = w_kv_b.reshape(MLA_KV_RANK, MLA_HEADS, MLA_NOPE + MLA_V)
    w_ukT = jnp.transpose(wkv[:, :, :MLA_NOPE], (1, 2, 0)).astype(BF16)
    w_uv = jnp.transpose(wkv[:, :, MLA_NOPE:], (1, 0, 2)).astype(BF16)

    inv = ROPE_THETA ** (-jnp.arange(half, dtype=F32) / half)
    inv_slot = jnp.tile(inv, 2 * MLA_ROPE // half).reshape(1, 2 * MLA_ROPE)
    sgn_slot = jnp.concatenate([jnp.ones((MLA_ROPE,), F32), -jnp.ones((half,), F32),
                                jnp.ones((half,), F32)]).reshape(1, 2 * MLA_ROPE)

    w_ckv = jnp.concatenate([w_ck, w_cv], axis=1).astype(BF16)
    wqT = peer_w_q.T.astype(BF16)
    keys2 = peer_keys.reshape(PEER_HEADS * 2, N_KEYS, PEER_HALF).astype(BF16)
    uT = peer_u.T.astype(BF16)
    v_b = peer_v.astype(BF16)

    k_mem, v_mem = _mem_kv(mem, row(norm_mem), w_ckv)
    kb = _tile(S, 512)
    qcat, kcat, ct_blk, qs, ks, vs = _proj(x2d, pos2d, inv_slot, sgn_slot, row(norm_mix), w_in_r,
                                           row(q_a_norm), w_qb_r, row(kv_a_norm), w_ukT, kb)
    o_lat = _mla(qcat, kcat.reshape(-1, kb, MLA_QK), ct_blk, B, _tile(kb, 256))
    o_b = _swa(swa_sinks, qs, ks, vs, B, S, _tile(S, 512))
    x2, hf, scT = _mid(x2d, o_lat, o_b, k_mem, v_mem, w_uv, row(out_norm_mla), row(out_norm_swa),
                       w_out.astype(BF16), row(norm_cross), w_cq.astype(BF16), w_co.astype(BF16),
                       row(norm_ffn), wqT, keys2, S, _tile(S, 512))
    return hf, scT, uT, v_b, x2


def kernel(x, mem, positions, norm_mix, w_in, q_a_norm, w_q_b, kv_a_norm, w_kv_b, swa_sinks,
           out_norm_mla, out_norm_swa, w_out, norm_cross, norm_mem, w_cq, w_ck, w_cv, w_co,
           norm_ffn, peer_w_q, peer_keys, peer_u, peer_v, norm_final):
    B, S, D = x.shape
    depth = norm_mix.shape[0]
    assert depth == 1, "the final rmsnorm is fused into the last layer's PEER kernel"
    x2d = x.reshape(B * S, D)
    pos2d = positions.reshape(B * S, 1)
    l = 0
    hf, scT, uT, v_b, x2 = _layer(
        x2d, mem, pos2d, B, S, norm_mix[l], w_in[l], q_a_norm[l], w_q_b[l], kv_a_norm[l], w_kv_b[l],
        swa_sinks[l], out_norm_mla[l], out_norm_swa[l], w_out[l], norm_cross[l], norm_mem[l],
        w_cq[l], w_ck[l], w_cv[l], w_co[l], norm_ffn[l], peer_w_q[l], peer_keys[l], peer_u[l],
        peer_v[l])
    tile = _tile(S, 512)
    sel0 = _topk_first_tile(scT, tile)
    out = _peer(hf, *sel0, scT, uT, v_b, x2, norm_final.reshape(1, D), tile, 8)
    return out.reshape(B, S, D)
```

```python
import functools
import math

import jax
import jax.numpy as jnp
from jax import lax
from jax.experimental import pallas as pl
from jax.experimental.pallas import tpu as pltpu

F32 = jnp.float32
BF16 = jnp.bfloat16
NEG_INF = float("-inf")
LOG2E = math.log2(math.e)

EPS = 1e-6
ROPE_THETA = 10000.0
MLA_HEADS = 4
MLA_NOPE = 128
MLA_ROPE = 64
MLA_V = 128
MLA_Q_RANK = 256
MLA_KV_RANK = 128
MLA_QK = MLA_KV_RANK + MLA_ROPE
MLA_CT_ROWS = MLA_KV_RANK + 16
SWA_HEADS = 8
SWA_KV_HEADS = 2
SWA_HD = 64
SWA_BLK = 128
X_HEADS = 4
X_HD = 128
PEER_HEADS = 8
N_KEYS = 128
PEER_HALF = 128
PEER_TOPK = 16

LANES = 128
VMEM_LIMIT = 48 * 1024 * 1024
PEER_VMEM_LIMIT = 56 * 1024 * 1024


def _rms(x, g):
    return x * lax.rsqrt(jnp.mean(x * x, axis=-1, keepdims=True) + EPS) * g


def _dot(a, b):
    return jnp.dot(a, b, preferred_element_type=F32)


def _dot_nt(a, b):
    return lax.dot_general(a, b, (((1,), (1,)), ((), ())), preferred_element_type=F32)


def _const_spec(shape):
    zeros = (0,) * len(shape)
    return pl.BlockSpec(shape, lambda *_: zeros)


def _params(*sem):
    return pltpu.CompilerParams(dimension_semantics=sem, vmem_limit_bytes=VMEM_LIMIT)


def _memkv_kernel(mem_ref, g_ref, w_ref, k_ref, v_ref):
    mn = _rms(mem_ref[0], g_ref[...]).astype(BF16)
    kv = _dot(mn, w_ref[...])
    width = k_ref.shape[-1]
    k_ref[0] = kv[:, :width].astype(BF16)
    v_ref[0] = kv[:, width:].astype(BF16)


def _mem_kv(mem, norm_mem, w_ckv):
    B, M, D = mem.shape
    width = w_ckv.shape[1] // 2
    return pl.pallas_call(
        _memkv_kernel,
        grid=(B,),
        in_specs=[pl.BlockSpec((1, M, D), lambda b: (b, 0, 0)),
                  _const_spec((1, D)), _const_spec(w_ckv.shape)],
        out_specs=[pl.BlockSpec((1, M, width), lambda b: (b, 0, 0))] * 2,
        out_shape=[jax.ShapeDtypeStruct((B, M, width), BF16)] * 2,
        compiler_params=_params("arbitrary"),
        name="mem_kv",
    )(mem, norm_mem, w_ckv)


_C_CQ = 0
_C_CKV = _C_CQ + MLA_Q_RANK
_C_QS = _C_CKV + MLA_KV_RANK
_C_KS = _C_QS + SWA_HEADS * LANES
_C_VS = _C_KS + SWA_KV_HEADS * SWA_HD
_C_KR = _C_VS + SWA_KV_HEADS * SWA_HD
_C_END = _C_KR + 2 * MLA_ROPE


def _proj_kernel(x_ref, pos_ref, inv_ref, sgn_ref, nmix_ref, win_ref, qan_ref, wqb_ref, kvan_ref,
                 wuk_ref, qcat_ref, kcat_ref, ct_ref, qs_ref, ks_ref, vs_ref):
    h = _rms(x_ref[...], nmix_ref[...]).astype(BF16)
    proj = _dot(h, win_ref[...])
    for hh in range(SWA_HEADS):
        slot = proj[:, _C_QS + hh * LANES:_C_QS + (hh + 1) * LANES]
        qs_ref[hh] = (slot * (float(SWA_HD) ** -0.5)).astype(BF16)
    ks_ref[...] = proj[:, _C_KS:_C_VS].astype(BF16)
    vs_ref[...] = proj[:, _C_VS:_C_KR].astype(BF16)

    ang = pos_ref[...].astype(F32) * inv_ref[...]
    lane = lax.broadcasted_iota(jnp.int32, ang.shape, 1)
    cs = jnp.where(lane < MLA_ROPE, jnp.cos(ang), jnp.sin(ang) * sgn_ref[...])

    def rope_slot(slot):
        r = slot * cs
        return (r + pltpu.roll(r, MLA_ROPE, 1))[:, :MLA_ROPE]

    c = _rms(proj[:, _C_CKV:_C_QS], kvan_ref[...])
    k_r = rope_slot(proj[:, _C_KR:_C_END])
    kcat_ref[...] = jnp.concatenate([c, k_r], axis=1).astype(BF16)
    ct_ref[0, 0:MLA_KV_RANK, :] = c.T.astype(BF16)
    ct_ref[0, MLA_KV_RANK:, :] = jnp.ones((MLA_CT_ROWS - MLA_KV_RANK, c.shape[0]), BF16)

    qn = _rms(proj[:, _C_CQ:_C_CKV], qan_ref[...]).astype(BF16)
    q2 = _dot(qn, wqb_ref[...])
    scale = float(MLA_NOPE + MLA_ROPE) ** -0.5 * LOG2E
    rope_base = MLA_HEADS * MLA_NOPE
    for hd in range(MLA_HEADS):
        q_lat = _dot(q2[:, hd * MLA_NOPE:(hd + 1) * MLA_NOPE].astype(BF16), wuk_ref[hd])
        q_r = rope_slot(q2[:, rope_base + hd * LANES: rope_base + (hd + 1) * LANES])
        qcat_ref[hd] = (jnp.concatenate([q_lat, q_r], axis=1) * scale).astype(BF16)


def _proj(x2d, pos2d, inv_slot, sgn_slot, norm_mix, w_in_r, q_a_norm, w_qb_r, kv_a_norm, w_ukT, tile):
    T, D = x2d.shape
    kv_w = SWA_KV_HEADS * SWA_HD
    row = lambda i: (i, 0)
    return pl.pallas_call(
        _proj_kernel,
        grid=(T // tile,),
        in_specs=[pl.BlockSpec((tile, D), row), pl.BlockSpec((tile, 1), row),
                  _const_spec(inv_slot.shape), _const_spec(sgn_slot.shape),
                  _const_spec(norm_mix.shape), _const_spec(w_in_r.shape),
                  _const_spec(q_a_norm.shape), _const_spec(w_qb_r.shape),
                  _const_spec(kv_a_norm.shape), _const_spec(w_ukT.shape)],
        out_specs=[pl.BlockSpec((MLA_HEADS, tile, MLA_QK), lambda i: (0, i, 0)),
                   pl.BlockSpec((tile, MLA_QK), row),
                   pl.BlockSpec((1, MLA_CT_ROWS, tile), lambda i: (i, 0, 0)),
                   pl.BlockSpec((SWA_HEADS, tile, LANES), lambda i: (0, i, 0)),
                   pl.BlockSpec((tile, kv_w), row),
                   pl.BlockSpec((tile, kv_w), row)],
        out_shape=[jax.ShapeDtypeStruct((MLA_HEADS, T, MLA_QK), BF16),
                   jax.ShapeDtypeStruct((T, MLA_QK), BF16),
                   jax.ShapeDtypeStruct((T // tile, MLA_CT_ROWS, tile), BF16),
                   jax.ShapeDtypeStruct((SWA_HEADS, T, LANES), BF16),
                   jax.ShapeDtypeStruct((T, kv_w), BF16),
                   jax.ShapeDtypeStruct((T, kv_w), BF16)],
        compiler_params=_params("arbitrary"),
        name="proj",
    )(x2d, pos2d, inv_slot, sgn_slot, norm_mix, w_in_r, q_a_norm, w_qb_r, kv_a_norm, w_ukT)


def _mla_kernel(q_ref, k_ref, ct_ref, o_ref, sa_ref, sb_ref, m_ref, acc_ref, *, tq, kb):
    i = pl.program_id(1)
    rows = MLA_HEADS * tq
    q = q_ref[...].reshape(rows, MLA_QK)
    m_ref[...] = jnp.full(m_ref.shape, NEG_INF, F32)
    acc_ref[...] = jnp.zeros(acc_ref.shape, F32)

    def scores(j, s_ref):
        s_ref[...] = _dot_nt(k_ref[j], q)

    def update(j, s_ref, masked):
        s = s_ref[...]
        if masked:
            q_idx = i * tq + (lax.broadcasted_iota(jnp.int32, s.shape, 1) & (tq - 1))
            k_idx = j * kb + lax.broadcasted_iota(jnp.int32, s.shape, 0)
            s = jnp.where(k_idx <= q_idx, s, NEG_INF)
        m_prev = m_ref[...]
        m_new = jnp.maximum(m_prev, jnp.max(s, axis=0, keepdims=True))
        alpha = jnp.exp2(m_prev - m_new)
        p = jnp.exp2(s - m_new)
        acc_ref[...] = alpha * acc_ref[...] + _dot(ct_ref[j], p.astype(BF16))
        m_ref[...] = m_new

    n_full = (i * tq) // kb
    scores(0, sa_ref)

    def body(t, carry):
        j = 2 * t
        scores(j + 1, sb_ref)
        update(j, sa_ref, False)
        scores(j + 2, sa_ref)
        update(j + 1, sb_ref, False)
        return carry

    lax.fori_loop(0, n_full // 2, body, 0)

    @pl.when(n_full % 2 == 0)
    def _():
        update(n_full, sa_ref, True)

    @pl.when(n_full % 2 == 1)
    def _():
        scores(n_full, sb_ref)
        update(n_full - 1, sa_ref, False)
        update(n_full, sb_ref, True)

    o = acc_ref[0:MLA_KV_RANK, :] / acc_ref[MLA_KV_RANK:MLA_KV_RANK + 1, :]
    for hd in range(MLA_HEADS):
        o_ref[:, hd * MLA_KV_RANK:(hd + 1) * MLA_KV_RANK] = o[:, hd * tq:(hd + 1) * tq].T.astype(BF16)


def _mla(qcat, kcat_blk, ct_blk, B, tq):
    nkb, kb, _ = kcat_blk.shape
    S = nkb * kb // B
    nq = S // tq
    return pl.pallas_call(
        functools.partial(_mla_kernel, tq=tq, kb=kb),
        grid=(B, nq),
        in_specs=[pl.BlockSpec((MLA_HEADS, tq, MLA_QK), lambda b, i: (0, b * nq + i, 0)),
                  pl.BlockSpec((nkb // B, kb, MLA_QK), lambda b, i: (b, 0, 0)),
                  pl.BlockSpec((nkb // B, MLA_CT_ROWS, kb), lambda b, i: (b, 0, 0))],
        out_specs=pl.BlockSpec((tq, MLA_HEADS * MLA_KV_RANK), lambda b, i: (b * nq + i, 0)),
        out_shape=jax.ShapeDtypeStruct((B * S, MLA_HEADS * MLA_KV_RANK), BF16),
        scratch_shapes=[pltpu.VMEM((kb, MLA_HEADS * tq), F32), pltpu.VMEM((kb, MLA_HEADS * tq), F32),
                        pltpu.VMEM((1, MLA_HEADS * tq), F32),
                        pltpu.VMEM((MLA_CT_ROWS, MLA_HEADS * tq), F32)],
        compiler_params=_params("arbitrary", "arbitrary"),
        name="mla",
    )(qcat, kcat_blk, ct_blk)


def _swa_kernel(sink_ref, bp_ref, bc_ref, q_ref, k_ref, kp_ref, v_ref, vp_ref, o_ref, kf_ref, vf_ref,
                *, ts):
    i = pl.program_id(1)
    kv_w = SWA_KV_HEADS * SWA_HD
    kf_ref[0:SWA_BLK] = kp_ref[...]
    kf_ref[SWA_BLK:] = k_ref[...]
    vf_ref[0:SWA_BLK, 0:kv_w] = vp_ref[...]
    vf_ref[SWA_BLK:, 0:kv_w] = v_ref[...]
    vf_ref[:, kv_w:] = jnp.ones((ts + SWA_BLK, LANES), BF16)
    low_half = lax.broadcasted_iota(jnp.int32, (SWA_BLK, LANES), 1) < SWA_HD
    group = SWA_HEADS // SWA_KV_HEADS
    rows = SWA_HEADS * SWA_BLK
    sink = sink_ref[...]
    for n in range(ts // SWA_BLK):
        r0 = n * SWA_BLK
        q = q_ref[:, r0:r0 + SWA_BLK, :].reshape(rows, LANES)
        prev_pen = jnp.where(jnp.logical_and(i == 0, n == 0), NEG_INF, 0.0).astype(F32)
        sp = _dot_nt(q, kf_ref[r0:r0 + SWA_BLK]) + bp_ref[...] + prev_pen
        sc = _dot_nt(q, kf_ref[r0 + SWA_BLK:r0 + 2 * SWA_BLK]) + bc_ref[...]
        m = jnp.maximum(jnp.max(jnp.maximum(sp, sc), axis=-1, keepdims=True), sink)
        ep = jnp.exp(sp - m).astype(BF16)
        ec = jnp.exp(sc - m).astype(BF16)
        o_ext = _dot(ep, vf_ref[r0:r0 + SWA_BLK]) + _dot(ec, vf_ref[r0 + SWA_BLK:r0 + 2 * SWA_BLK])
        o = o_ext[:, :kv_w] / (o_ext[:, kv_w:] + jnp.exp(sink - m))
        for pair in range(SWA_HEADS // 2):
            kv = (2 * pair) // group
            oe = o[(2 * pair) * SWA_BLK:(2 * pair + 1) * SWA_BLK]
            oo = o[(2 * pair + 1) * SWA_BLK:(2 * pair + 2) * SWA_BLK]
            if kv == 0:
                both = jnp.where(low_half, oe, pltpu.roll(oo, SWA_HD, 1))
            else:
                both = jnp.where(low_half, pltpu.roll(oe, SWA_HD, 1), oo)
            o_ref[r0:r0 + SWA_BLK, pair * LANES:(pair + 1) * LANES] = both.astype(BF16)


def _swa_bias():
    r = jnp.arange(SWA_HEADS * SWA_BLK)
    a = (r % SWA_BLK)[:, None]
    slope = 2.0 ** (-(8.0 / SWA_HEADS) * ((r // SWA_BLK) + 1).astype(F32))[:, None]
    j = jnp.arange(SWA_BLK)[None, :]
    dist_cur = (a - j).astype(F32)
    bias_prev = jnp.where(j > a, -slope * (dist_cur + float(SWA_BLK)), NEG_INF)
    bias_cur = jnp.where(j <= a, -slope * dist_cur, NEG_INF)
    return bias_prev.astype(F32), bias_cur.astype(F32)


def _swa(sinks, qs, ks, vs, B, S, ts):
    T = B * S
    nt = S // ts
    kv_w = SWA_KV_HEADS * SWA_HD
    blk_per_tile = ts // SWA_BLK
    rows = SWA_HEADS * SWA_BLK
    bias_prev, bias_cur = _swa_bias()
    sink_rows = jnp.repeat(sinks.astype(F32), SWA_BLK).reshape(rows, 1)
    cur = lambda b, i: (b * nt + i, 0)
    prev = lambda b, i: (jnp.maximum((b * nt + i) * blk_per_tile - 1, 0), 0)
    return pl.pallas_call(
        functools.partial(_swa_kernel, ts=ts),
        grid=(B, nt),
        in_specs=[_const_spec((rows, 1)), _const_spec((rows, SWA_BLK)), _const_spec((rows, SWA_BLK)),
                  pl.BlockSpec((SWA_HEADS, ts, LANES), lambda b, i: (0, b * nt + i, 0)),
                  pl.BlockSpec((ts, kv_w), cur), pl.BlockSpec((SWA_BLK, kv_w), prev),
                  pl.BlockSpec((ts, kv_w), cur), pl.BlockSpec((SWA_BLK, kv_w), prev)],
        out_specs=pl.BlockSpec((ts, SWA_HEADS * SWA_HD), cur),
        out_shape=jax.ShapeDtypeStruct((T, SWA_HEADS * SWA_HD), BF16),
        scratch_shapes=[pltpu.VMEM((ts + SWA_BLK, kv_w), BF16),
                        pltpu.VMEM((ts + SWA_BLK, kv_w + LANES), BF16)],
        compiler_params=_params("arbitrary", "arbitrary"),
        name="swa",
    )(sink_rows, bias_prev, bias_cur, qs, ks, ks, vs, vs)


def _mid_kernel(x_ref, ol_ref, ob_ref, km_ref, vm_ref, wuv_ref, onm_ref, ons_ref, wout_ref,
                ncross_ref, wcq_ref, wco_ref, nffn_ref, wqT_ref, keys_ref,
                x2_ref, hf_ref, scT_ref):
    ol = ol_ref[...]
    o_a = jnp.concatenate(
        [_dot(ol[:, hd * MLA_KV_RANK:(hd + 1) * MLA_KV_RANK], wuv_ref[hd]) for hd in range(MLA_HEADS)],
        axis=1)
    mix = jnp.concatenate([_rms(o_a, onm_ref[...]), _rms(ob_ref[...].astype(F32), ons_ref[...])],
                          axis=1).astype(BF16)
    x1 = x_ref[...] + _dot(mix, wout_ref[...])

    hc = _rms(x1, ncross_ref[...]).astype(BF16)
    q = _dot(hc, wcq_ref[...]) * (float(X_HD) ** -0.5)
    heads = []
    for hd in range(X_HEADS):
        sl = slice(hd * X_HD, (hd + 1) * X_HD)
        s = _dot_nt(q[:, sl].astype(BF16), km_ref[0, :, sl])
        e = jnp.exp(s - jnp.max(s, axis=-1, keepdims=True))
        p = e / jnp.sum(e, axis=-1, keepdims=True)
        heads.append(_dot(p.astype(BF16), vm_ref[0, :, sl]))
    x2 = x1 + _dot(jnp.concatenate(heads, axis=1).astype(BF16), wco_ref[...])
    x2_ref[...] = x2

    hf = _rms(x2, nffn_ref[...]).astype(BF16)
    hf_ref[...] = hf
    qpT = _dot_nt(wqT_ref[...], hf)
    for hc_i in range(PEER_HEADS * 2):
        sl = slice(hc_i * PEER_HALF, (hc_i + 1) * PEER_HALF)
        scT_ref[sl, :] = _dot(keys_ref[hc_i], qpT[sl].astype(BF16))


def _mid(x2d, o_lat, o_b, k_mem, v_mem, w_uv, onm, ons, w_out, ncross, w_cq, w_co, nffn, wqT, keys2,
         S, tile):
    T, D = x2d.shape
    row = lambda i: (i, 0)
    mem_idx = lambda i: ((i * tile) // S, 0, 0)
    n_sc = PEER_HEADS * 2 * N_KEYS
    consts = [w_uv, onm, ons, w_out, ncross, w_cq, w_co, nffn, wqT, keys2]
    return pl.pallas_call(
        _mid_kernel,
        grid=(T // tile,),
        in_specs=[pl.BlockSpec((tile, D), row),
                  pl.BlockSpec((tile, o_lat.shape[1]), row),
                  pl.BlockSpec((tile, o_b.shape[1]), row),
                  pl.BlockSpec((1,) + k_mem.shape[1:], mem_idx),
                  pl.BlockSpec((1,) + v_mem.shape[1:], mem_idx)]
                 + [_const_spec(c.shape) for c in consts],
        out_specs=[pl.BlockSpec((tile, D), row), pl.BlockSpec((tile, D), row),
                   pl.BlockSpec((n_sc, tile), lambda i: (0, i))],
        out_shape=[jax.ShapeDtypeStruct((T, D), F32), jax.ShapeDtypeStruct((T, D), BF16),
                   jax.ShapeDtypeStruct((n_sc, T), F32)],
        compiler_params=_params("arbitrary"),
        name="mid",
    )(x2d, o_lat, o_b, k_mem, v_mem, *consts)


_CAND_ROWS = [(0, 0), (0, 8)] + [(a, 0) for a in range(1, 8)]


def _topk_head_pair(load, tk, between=(None, None), lockstep=True):
    key_iota = lax.broadcasted_iota(jnp.int32, (N_KEYS, tk), 0)
    k_iota = lax.broadcasted_iota(jnp.int32, (PEER_TOPK, tk), 0)
    sub8 = lax.broadcasted_iota(jnp.int32, (8, tk), 0)
    pos = jnp.concatenate([a * PEER_TOPK + b0 + sub8 for a, b0 in _CAND_ROWS]
                          + [(sub8 + 8) * PEER_TOPK], axis=0)

    zero_s = jnp.zeros((PEER_TOPK, tk), F32)
    zero_i = jnp.zeros((PEER_TOPK, tk), jnp.int32)

    def extract(problems, tie_keys, n_tie):
        def body(k, carry):
            sel = k_iota == k
            out = []
            for (vals, s_out, i_out), tkey in zip(carry, tie_keys):
                m = jnp.max(vals, axis=0, keepdims=True)
                c = jnp.where(vals == m, tkey, n_tie)
                first = jnp.min(c, axis=0, keepdims=True)
                out.append((jnp.where(c == first, NEG_INF, vals),
                            jnp.where(sel, m, s_out), jnp.where(sel, first, i_out)))
            return tuple(out)

        init = tuple((v, zero_s, zero_i) for v in problems)
        res = lax.fori_loop(0, PEER_TOPK, body, init, unroll=True)
        return [(s, i) for _, s, i in res]

    def candidates(s0, i0, s1, i1):
        cs = jnp.concatenate([s0[a:a + 1] + s1[b0:b0 + 8] for a, b0 in _CAND_ROWS]
                             + [s0[8:16] + s1[0:1]], axis=0)
        ci = jnp.concatenate([i0[a:a + 1] * N_KEYS + i1[b0:b0 + 8] for a, b0 in _CAND_ROWS]
                             + [i0[8:16] * N_KEYS + i1[0:1]], axis=0)
        return cs, pos * (N_KEYS * N_KEYS) + ci

    def extract_all(problems, tie_keys, n_tie):
        if lockstep:
            return extract(problems, tie_keys, n_tie)
        return [extract([p], [t], n_tie)[0] for p, t in zip(problems, tie_keys)]

    halves = []
    for h in range(2):
        halves.append(extract_all([load(2 * h), load(2 * h + 1)], [key_iota, key_iota], N_KEYS))
        if between[h] is not None:
            between[h]()
    cands = [candidates(s0, i0, s1, i1) for (s0, i0), (s1, i1) in halves]
    best = extract_all([c[0] for c in cands], [c[1] for c in cands], 1 << 30)
    out = []
    for bs, key in best:
        bi = key & (N_KEYS * N_KEYS - 1)
        e = jnp.exp(bs - jnp.max(bs, axis=0, keepdims=True))
        g = e / jnp.sum(e, axis=0, keepdims=True)
        out.append((g, (bi >> 7).astype(F32), (bi & (N_KEYS - 1)).astype(F32)))
    return out


_PAIR_ROWS = 2 * 2 * N_KEYS
_PAIR_SEL = 2 * PEER_TOPK


def _topk_kernel(sc_ref, i_ref, j_ref, g_ref, is_ref, js_ref, gs_ref):
    tk = sc_ref.shape[1]

    def head_pair(hp, carry):
        base = pl.multiple_of(hp * _PAIR_ROWS, _PAIR_ROWS)
        res = _topk_head_pair(lambda r: sc_ref[pl.ds(base + r * N_KEYS, N_KEYS), :], tk)
        for h, (g, fi, fj) in enumerate(res):
            rows = pl.ds(pl.multiple_of(hp * _PAIR_SEL + h * PEER_TOPK, PEER_TOPK), PEER_TOPK)
            gs_ref[rows, :] = g
            is_ref[rows, :] = fi
            js_ref[rows, :] = fj
        return carry

    lax.fori_loop(0, PEER_HEADS // 2, head_pair, 0)
    i_ref[...] = is_ref[...].T
    j_ref[...] = js_ref[...].T
    g_ref[...] = gs_ref[...].T


def _topk_first_tile(scT, tile):
    n_sc = scT.shape[0]
    n_sel = PEER_HEADS * PEER_TOPK
    out = jax.ShapeDtypeStruct((tile, n_sel), F32)
    return pl.pallas_call(
        _topk_kernel,
        grid=(tile // LANES,),
        in_specs=[pl.BlockSpec((n_sc, LANES), lambda i: (0, i))],
        out_specs=[pl.BlockSpec((LANES, n_sel), lambda i: (i, 0))] * 3,
        out_shape=[out] * 3,
        scratch_shapes=[pltpu.VMEM((n_sel, LANES), F32)] * 3,
        compiler_params=_params("arbitrary"),
        name="topk",
    )(scT)


_W_PAD = 8
_I_SPLIT = 2


def _peer_kernel(hf_ref, i0_ref, j0_ref, g0_ref, sc_ref, uT_ref, v_ref, x2_ref, nf_ref, o_ref,
                 w_ref, stash_ref, y_ref, i_ref, j_ref, g_ref, in_ref, jn_ref, gn_ref, *, tile, n_blk):
    t_idx = pl.program_id(0)
    half = pl.program_id(1)
    s = pl.program_id(2)
    pitch = tile + _W_PAD
    n_i = N_KEYS // _I_SPLIT
    first_step = jnp.logical_and(half == 0, s == 0)

    @pl.when(jnp.logical_and(first_step, t_idx == 0))
    def _():
        i_ref[...] = i0_ref[...]
        j_ref[...] = j0_ref[...]
        g_ref[...] = g0_ref[...]

    @pl.when(jnp.logical_and(first_step, t_idx > 0))
    def _():
        for lb in range(tile // LANES):
            rows = slice(lb * LANES, (lb + 1) * LANES)
            i_ref[rows, :] = in_ref[lb].T
            j_ref[rows, :] = jn_ref[lb].T
            g_ref[rows, :] = gn_ref[lb].T

    @pl.when(first_step)
    def _():
        sub = lax.broadcasted_iota(jnp.int32, (N_KEYS, N_KEYS), 0).astype(F32)

        def tok(t, carry):
            row = pl.ds(t, 1)
            a_t = jnp.where(sub == i_ref[row, :], 0.5 * g_ref[row, :], 0.0).astype(BF16)
            b_t = jnp.where(sub == j_ref[row, :], 1.0, 0.0).astype(BF16)
            w = _dot_nt(a_t, b_t)
            w_ref[pl.ds(t, n_i, stride=pitch), :] = w[:n_i]
            stash_ref[pl.ds(pl.multiple_of(t * n_i, n_i), n_i), :] = w[n_i:].astype(BF16)
            return carry

        lax.fori_loop(0, tile, tok, 0, unroll=64)
        y_ref[...] = jnp.zeros(y_ref.shape, F32)

    @pl.when(jnp.logical_and(half == 1, s == 0))
    def _():
        def tok(t, carry):
            rows = pl.ds(pl.multiple_of(t * n_i, n_i), n_i)
            w_ref[pl.ds(t, n_i, stride=pitch), :] = stash_ref[rows, :].astype(F32)
            return carry

        lax.fori_loop(0, tile, tok, 0, unroll=64)

    hf = hf_ref[...]
    gated = []

    def expert_blocks(c0, c1):
        for c in range(c0, c1, 2):
            act = _dot(hf, uT_ref[:, c * N_KEYS:(c + 2) * N_KEYS])
            for d in range(2):
                a = act[:, d * N_KEYS:(d + 1) * N_KEYS]
                w = w_ref[pl.ds(pl.multiple_of((s * n_blk + c + d) * pitch, 8), tile), :]
                gated.append((a * (1.0 + lax.erf(a * (0.5 ** 0.5))) * w).astype(BF16))

    u = half * pl.num_programs(2) + s
    pairs = PEER_HEADS // 2
    lb = u // pairs
    res = _topk_head_pair(lambda r: sc_ref[r * N_KEYS:(r + 1) * N_KEYS, :], LANES,
                          between=(lambda: expert_blocks(0, n_blk // 2),
                                   lambda: expert_blocks(n_blk // 2, n_blk)), lockstep=False)
    y_ref[...] += _dot(jnp.concatenate(gated, axis=1), v_ref[...])

    for h, (g, fi, fj) in enumerate(res):
        rows = pl.ds(pl.multiple_of((u % pairs) * _PAIR_SEL + h * PEER_TOPK, PEER_TOPK), PEER_TOPK)
        gn_ref[lb, rows, :] = g
        in_ref[lb, rows, :] = fi
        jn_ref[lb, rows, :] = fj

    @pl.when(jnp.logical_and(half == _I_SPLIT - 1, s == pl.num_programs(2) - 1))
    def _():
        o_ref[...] = _rms(x2_ref[...] + y_ref[...], nf_ref[...])


def _peer(hf, sel0_i, sel0_j, sel0_g, scT, uT, v, x2, norm_final, tile, n_blk):
    T, D = x2.shape
    n_exp = v.shape[0]
    n_sel = sel0_i.shape[1]
    ew = n_blk * N_KEYS
    steps = n_exp // ew // _I_SPLIT
    assert steps * n_blk * _I_SPLIT == N_KEYS
    n_tiles = T // tile
    blocks = tile // LANES
    pairs = PEER_HEADS // 2
    assert _I_SPLIT * steps == pairs * blocks, "one retrieval unit of the next tile per grid step"
    row = lambda t, h, s: (t, 0)

    def next_scores(t, h, s):
        u = h * steps + s
        return (u % pairs, jnp.minimum(t + 1, n_tiles - 1) * blocks + u // pairs)

    params = pltpu.CompilerParams(dimension_semantics=("arbitrary",) * 3,
                                  vmem_limit_bytes=PEER_VMEM_LIMIT)
    sel_scratch = [pltpu.VMEM((tile, n_sel), F32)] * 3 + [pltpu.VMEM((blocks, n_sel, LANES), F32)] * 3
    return pl.pallas_call(
        functools.partial(_peer_kernel, tile=tile, n_blk=n_blk),
        grid=(n_tiles, _I_SPLIT, steps),
        in_specs=[pl.BlockSpec((tile, D), row),
                  _const_spec((tile, n_sel)), _const_spec((tile, n_sel)), _const_spec((tile, n_sel)),
                  pl.BlockSpec((_PAIR_ROWS, LANES), next_scores),
                  pl.BlockSpec((D, ew), lambda t, h, s: (0, h * steps + s)),
                  pl.BlockSpec((ew, D), lambda t, h, s: (h * steps + s, 0)),
                  pl.BlockSpec((tile, D), row),
                  _const_spec(norm_final.shape)],
        out_specs=pl.BlockSpec((tile, D), row),
        out_shape=jax.ShapeDtypeStruct((T, D), F32),
        scratch_shapes=[pltpu.VMEM((N_KEYS // _I_SPLIT * (tile + _W_PAD), N_KEYS), F32),
                        pltpu.VMEM((tile * N_KEYS // _I_SPLIT, N_KEYS), BF16),
                        pltpu.VMEM((tile, D), F32)] + sel_scratch,
        compiler_params=params,
        name="peer",
    )(hf, sel0_i, sel0_j, sel0_g, scT, uT, v, x2, norm_final)


def _tile(n, want):
    t = min(n, want)
    assert n % t == 0, (n, t)
    return t


def _tiles(S):
    kb = _tile(S, 512)
    return dict(proj=kb, mla_keys=kb, mla_queries=_tile(kb, 256), swa=_tile(S, 512),
                mid=_tile(S, 512), peer=_tile(S, 512),
                peer_blocks=8)


def _layer(x2d, mem, pos2d, B, S, norm_mix, w_in, q_a_norm, w_q_b, kv_a_norm, w_kv_b, swa_sinks,
           out_norm_mla, out_norm_swa, w_out, norm_cross, norm_mem, w_cq, w_ck, w_cv, w_co,
           norm_ffn, peer_w_q, peer_keys, peer_u, peer_v):
    D = x2d.shape[1]
    row = lambda g: g.reshape(1, -1)

    o = MLA_Q_RANK + MLA_KV_RANK
    w_kr = w_in[:, o:o + MLA_ROPE]
    o += MLA_ROPE
    w_qs = w_in[:, o:o + SWA_HEADS * SWA_HD].reshape(D, SWA_HEADS, SWA_HD)
    o += SWA_HEADS * SWA_HD
    w_ks = w_in[:, o:o + SWA_KV_HEADS * SWA_HD]
    o += SWA_KV_HEADS * SWA_HD
    w_vs = w_in[:, o:o + SWA_KV_HEADS * SWA_HD]
    group = SWA_HEADS // SWA_KV_HEADS
    zeros = jnp.zeros((D, SWA_HD), w_in.dtype)
    qs_slots = [jnp.concatenate([w_qs[:, hh], zeros] if hh // group == 0 else [zeros, w_qs[:, hh]], axis=1)
                for hh in range(SWA_HEADS)]
    half = MLA_ROPE // 2
    swap = lambda w: jnp.concatenate([w[:, half:], w[:, :half]], axis=1)
    w_in_r = jnp.concatenate([w_in[:, :MLA_Q_RANK + MLA_KV_RANK]] + qs_slots
                             + [w_ks, w_vs, w_kr, swap(w_kr)], axis=1).astype(BF16)

    wq = w_q_b.reshape(MLA_Q_RANK, MLA_HEADS, MLA_NOPE + MLA_ROPE)
    q_nope = wq[:, :, :MLA_NOPE].reshape(MLA_Q_RANK, MLA_HEADS * MLA_NOPE)
    q_rope = [jnp.concatenate([wq[:, hd, MLA_NOPE:], swap(wq[:, hd, MLA_NOPE:])], axis=1)
              for hd in range(MLA_HEADS)]
    w_qb_r = jnp.concatenate([q_nope] + q_rope, axis=1).astype(BF16)

    wkv = w_kv_b.reshape(MLA_KV_RANK, MLA_HEADS, MLA_NOPE + MLA_V)
    w_ukT = jnp.transpose(wkv[:, :, :MLA_NOPE], (1, 2, 0)).astype(BF16)
    w_uv = jnp.transpose(wkv[:, :, MLA_NOPE:], (1, 0, 2)).astype(BF16)

    inv = ROPE_THETA ** (-jnp.arange(half, dtype=F32) / half)
    inv_slot = jnp.tile(inv, 2 * MLA_ROPE // half).reshape(1, 2 * MLA_ROPE)
    sgn_slot = jnp.concatenate([jnp.ones((MLA_ROPE,), F32), -jnp.ones((half,), F32),
                                jnp.ones((half,), F32)]).reshape(1, 2 * MLA_ROPE)

    w_ckv = jnp.concatenate([w_ck, w_cv], axis=1).astype(BF16)
    wqT = peer_w_q.T.astype(BF16)
    keys2 = peer_keys.reshape(PEER_HEADS * 2, N_KEYS, PEER_HALF).astype(BF16)
    uT = peer_u.T.astype(BF16)
    v_b = peer_v.astype(BF16)

    tiles = _tiles(S)
    k_mem, v_mem = _mem_kv(mem, row(norm_mem), w_ckv)
    qcat, kcat, ct_blk, qs, ks, vs = _proj(x2d, pos2d, inv_slot, sgn_slot, row(norm_mix), w_in_r,
                                           row(q_a_norm), w_qb_r, row(kv_a_norm), w_ukT, tiles["proj"])
    o_lat = _mla(qcat, kcat.reshape(-1, tiles["mla_keys"], MLA_QK), ct_blk, B, tiles["mla_queries"])
    o_b = _swa(swa_sinks, qs, ks, vs, B, S, tiles["swa"])
    x2, hf, scT = _mid(x2d, o_lat, o_b, k_mem, v_mem, w_uv, row(out_norm_mla), row(out_norm_swa),
                       w_out.astype(BF16), row(norm_cross), w_cq.astype(BF16), w_co.astype(BF16),
                       row(norm_ffn), wqT, keys2, S, tiles["mid"])
    return hf, scT, uT, v_b, x2


def kernel(x, mem, positions, norm_mix, w_in, q_a_norm, w_q_b, kv_a_norm, w_kv_b, swa_sinks,
           out_norm_mla, out_norm_swa, w_out, norm_cross, norm_mem, w_cq, w_ck, w_cv, w_co,
           norm_ffn, peer_w_q, peer_keys, peer_u, peer_v, norm_final):
    B, S, D = x.shape
    depth = norm_mix.shape[0]
    assert depth == 1, "the final rmsnorm is fused into the last layer's PEER kernel"
    x2d = x.reshape(B * S, D)
    pos2d = positions.reshape(B * S, 1)
    l = 0
    hf, scT, uT, v_b, x2 = _layer(
        x2d, mem, pos2d, B, S, norm_mix[l], w_in[l], q_a_norm[l], w_q_b[l], kv_a_norm[l], w_kv_b[l],
        swa_sinks[l], out_norm_mla[l], out_norm_swa[l], w_out[l], norm_cross[l], norm_mem[l],
        w_cq[l], w_ck[l], w_cv[l], w_co[l], norm_ffn[l], peer_w_q[l], peer_keys[l], peer_u[l],
        peer_v[l])
    tiles = _tiles(S)
    sel0 = _topk_first_tile(scT, tiles["peer"])
    out = _peer(hf, *sel0, scT, uT, v_b, x2, norm_final.reshape(1, D), tiles["peer"],
                tiles["peer_blocks"])
    return out.reshape(B, S, D)
```

```python
import functools
import math

import jax
import jax.numpy as jnp
from jax import lax
from jax.experimental import pallas as pl
from jax.experimental.pallas import tpu as pltpu

F32 = jnp.float32
BF16 = jnp.bfloat16
NEG_INF = float("-inf")
LOG2E = math.log2(math.e)

EPS = 1e-6
ROPE_THETA = 10000.0
MLA_HEADS = 4
MLA_NOPE = 128
MLA_ROPE = 64
MLA_V = 128
MLA_Q_RANK = 256
MLA_KV_RANK = 128
MLA_QK = MLA_KV_RANK + MLA_ROPE
MLA_CT_ROWS = MLA_KV_RANK + 16
SWA_HEADS = 8
SWA_KV_HEADS = 2
SWA_HD = 64
SWA_BLK = 128
X_HEADS = 4
X_HD = 128
PEER_HEADS = 8
N_KEYS = 128
PEER_HALF = 128
PEER_TOPK = 16

LANES = 128
VMEM_LIMIT = 48 * 1024 * 1024
PEER_VMEM_LIMIT = 56 * 1024 * 1024


def _rms(x, g):
    return x * lax.rsqrt(jnp.mean(x * x, axis=-1, keepdims=True) + EPS) * g


def _dot(a, b):
    return jnp.dot(a, b, preferred_element_type=F32)


def _dot_nt(a, b):
    return lax.dot_general(a, b, (((1,), (1,)), ((), ())), preferred_element_type=F32)


def _const_spec(shape):
    zeros = (0,) * len(shape)
    return pl.BlockSpec(shape, lambda *_: zeros)


def _params(*sem):
    return pltpu.CompilerParams(dimension_semantics=sem, vmem_limit_bytes=VMEM_LIMIT)


def _memkv_kernel(mem_ref, g_ref, w_ref, k_ref, v_ref):
    mn = _rms(mem_ref[0], g_ref[...]).astype(BF16)
    kv = _dot(mn, w_ref[...])
    width = k_ref.shape[-1]
    k_ref[0] = kv[:, :width].astype(BF16)
    v_ref[0] = kv[:, width:].astype(BF16)


def _mem_kv(mem, norm_mem, w_ckv):
    B, M, D = mem.shape
    width = w_ckv.shape[1] // 2
    return pl.pallas_call(
        _memkv_kernel,
        grid=(B,),
        in_specs=[pl.BlockSpec((1, M, D), lambda b: (b, 0, 0)),
                  _const_spec((1, D)), _const_spec(w_ckv.shape)],
        out_specs=[pl.BlockSpec((1, M, width), lambda b: (b, 0, 0))] * 2,
        out_shape=[jax.ShapeDtypeStruct((B, M, width), BF16)] * 2,
        compiler_params=_params("arbitrary"),
        name="mem_kv",
    )(mem, norm_mem, w_ckv)


_C_CQ = 0
_C_CKV = _C_CQ + MLA_Q_RANK
_C_QS = _C_CKV + MLA_KV_RANK
_C_KS = _C_QS + SWA_HEADS * LANES
_C_VS = _C_KS + SWA_KV_HEADS * SWA_HD
_C_KR = _C_VS + SWA_KV_HEADS * SWA_HD
_C_END = _C_KR + 2 * MLA_ROPE


def _proj_kernel(x_ref, pos_ref, inv_ref, sgn_ref, nmix_ref, win_ref, qan_ref, wqb_ref, kvan_ref,
                 wuk_ref, qcat_ref, kcat_ref, ct_ref, qs_ref, ks_ref, vs_ref):
    h = _rms(x_ref[...], nmix_ref[...]).astype(BF16)
    proj = _dot(h, win_ref[...])
    for hh in range(SWA_HEADS):
        slot = proj[:, _C_QS + hh * LANES:_C_QS + (hh + 1) * LANES]
        qs_ref[hh] = (slot * (float(SWA_HD) ** -0.5)).astype(BF16)
    ks_ref[...] = proj[:, _C_KS:_C_VS].astype(BF16)
    vs_ref[...] = proj[:, _C_VS:_C_KR].astype(BF16)

    ang = pos_ref[...].astype(F32) * inv_ref[...]
    lane = lax.broadcasted_iota(jnp.int32, ang.shape, 1)
    cs = jnp.where(lane < MLA_ROPE, jnp.cos(ang), jnp.sin(ang) * sgn_ref[...])

    def rope_slot(slot):
        r = slot * cs
        return (r + pltpu.roll(r, MLA_ROPE, 1))[:, :MLA_ROPE]

    c = _rms(proj[:, _C_CKV:_C_QS], kvan_ref[...])
    k_r = rope_slot(proj[:, _C_KR:_C_END])
    kcat_ref[...] = jnp.concatenate([c, k_r], axis=1).astype(BF16)
    ct_ref[0, 0:MLA_KV_RANK, :] = c.T.astype(BF16)
    ct_ref[0, MLA_KV_RANK:, :] = jnp.ones((MLA_CT_ROWS - MLA_KV_RANK, c.shape[0]), BF16)

    qn = _rms(proj[:, _C_CQ:_C_CKV], qan_ref[...]).astype(BF16)
    q2 = _dot(qn, wqb_ref[...])
    scale = float(MLA_NOPE + MLA_ROPE) ** -0.5 * LOG2E
    rope_base = MLA_HEADS * MLA_NOPE
    for hd in range(MLA_HEADS):
        q_lat = _dot(q2[:, hd * MLA_NOPE:(hd + 1) * MLA_NOPE].astype(BF16), wuk_ref[hd])
        q_r = rope_slot(q2[:, rope_base + hd * LANES: rope_base + (hd + 1) * LANES])
        qcat_ref[hd] = (jnp.concatenate([q_lat, q_r], axis=1) * scale).astype(BF16)


def _proj(x2d, pos2d, inv_slot, sgn_slot, norm_mix, w_in_r, q_a_norm, w_qb_r, kv_a_norm, w_ukT, tile):
    T, D = x2d.shape
    kv_w = SWA_KV_HEADS * SWA_HD
    row = lambda i: (i, 0)
    return pl.pallas_call(
        _proj_kernel,
        grid=(T // tile,),
        in_specs=[pl.BlockSpec((tile, D), row), pl.BlockSpec((tile, 1), row),
                  _const_spec(inv_slot.shape), _const_spec(sgn_slot.shape),
                  _const_spec(norm_mix.shape), _const_spec(w_in_r.shape),
                  _const_spec(q_a_norm.shape), _const_spec(w_qb_r.shape),
                  _const_spec(kv_a_norm.shape), _const_spec(w_ukT.shape)],
        out_specs=[pl.BlockSpec((MLA_HEADS, tile, MLA_QK), lambda i: (0, i, 0)),
                   pl.BlockSpec((tile, MLA_QK), row),
                   pl.BlockSpec((1, MLA_CT_ROWS, tile), lambda i: (i, 0, 0)),
                   pl.BlockSpec((SWA_HEADS, tile, LANES), lambda i: (0, i, 0)),
                   pl.BlockSpec((tile, kv_w), row),
                   pl.BlockSpec((tile, kv_w), row)],
        out_shape=[jax.ShapeDtypeStruct((MLA_HEADS, T, MLA_QK), BF16),
                   jax.ShapeDtypeStruct((T, MLA_QK), BF16),
                   jax.ShapeDtypeStruct((T // tile, MLA_CT_ROWS, tile), BF16),
                   jax.ShapeDtypeStruct((SWA_HEADS, T, LANES), BF16),
                   jax.ShapeDtypeStruct((T, kv_w), BF16),
                   jax.ShapeDtypeStruct((T, kv_w), BF16)],
        compiler_params=_params("arbitrary"),
        name="proj",
    )(x2d, pos2d, inv_slot, sgn_slot, norm_mix, w_in_r, q_a_norm, w_qb_r, kv_a_norm, w_ukT)


def _mla_kernel(q_ref, k_ref, ct_ref, o_ref, sa_ref, sb_ref, m_ref, acc_ref, *, tq, kb):
    i = pl.program_id(1)
    rows = MLA_HEADS * tq
    q = q_ref[...].reshape(rows, MLA_QK)
    m_ref[...] = jnp.full(m_ref.shape, NEG_INF, F32)
    acc_ref[...] = jnp.zeros(acc_ref.shape, F32)

    def scores(j, s_ref):
        s_ref[...] = _dot_nt(k_ref[j], q)

    def update(j, s_ref, masked):
        s = s_ref[...]
        if masked:
            q_idx = i * tq + (lax.broadcasted_iota(jnp.int32, s.shape, 1) & (tq - 1))
            k_idx = j * kb + lax.broadcasted_iota(jnp.int32, s.shape, 0)
            s = jnp.where(k_idx <= q_idx, s, NEG_INF)
        m_prev = m_ref[...]
        m_new = jnp.maximum(m_prev, jnp.max(s, axis=0, keepdims=True))
        alpha = jnp.exp2(m_prev - m_new)
        p = jnp.exp2(s - m_new)
        acc_ref[...] = alpha * acc_ref[...] + _dot(ct_ref[j], p.astype(BF16))
        m_ref[...] = m_new

    n_full = (i * tq) // kb
    scores(0, sa_ref)

    def body(t, carry):
        j = 2 * t
        scores(j + 1, sb_ref)
        update(j, sa_ref, False)
        scores(j + 2, sa_ref)
        update(j + 1, sb_ref, False)
        return carry

    lax.fori_loop(0, n_full // 2, body, 0)

    @pl.when(n_full % 2 == 0)
    def _():
        update(n_full, sa_ref, True)

    @pl.when(n_full % 2 == 1)
    def _():
        scores(n_full, sb_ref)
        update(n_full - 1, sa_ref, False)
        update(n_full, sb_ref, True)

    o = acc_ref[0:MLA_KV_RANK, :] / acc_ref[MLA_KV_RANK:MLA_KV_RANK + 1, :]
    for hd in range(MLA_HEADS):
        o_ref[:, hd * MLA_KV_RANK:(hd + 1) * MLA_KV_RANK] = o[:, hd * tq:(hd + 1) * tq].T.astype(BF16)


def _mla(qcat, kcat_blk, ct_blk, B, tq):
    nkb, kb, _ = kcat_blk.shape
    S = nkb * kb // B
    nq = S // tq
    return pl.pallas_call(
        functools.partial(_mla_kernel, tq=tq, kb=kb),
        grid=(B, nq),
        in_specs=[pl.BlockSpec((MLA_HEADS, tq, MLA_QK), lambda b, i: (0, b * nq + i, 0)),
                  pl.BlockSpec((nkb // B, kb, MLA_QK), lambda b, i: (b, 0, 0)),
                  pl.BlockSpec((nkb // B, MLA_CT_ROWS, kb), lambda b, i: (b, 0, 0))],
        out_specs=pl.BlockSpec((tq, MLA_HEADS * MLA_KV_RANK), lambda b, i: (b * nq + i, 0)),
        out_shape=jax.ShapeDtypeStruct((B * S, MLA_HEADS * MLA_KV_RANK), BF16),
        scratch_shapes=[pltpu.VMEM((kb, MLA_HEADS * tq), F32), pltpu.VMEM((kb, MLA_HEADS * tq), F32),
                        pltpu.VMEM((1, MLA_HEADS * tq), F32),
                        pltpu.VMEM((MLA_CT_ROWS, MLA_HEADS * tq), F32)],
        compiler_params=_params("arbitrary", "arbitrary"),
        name="mla",
    )(qcat, kcat_blk, ct_blk)


def _swa_kernel(sink_ref, bp_ref, bc_ref, q_ref, k_ref, kp_ref, v_ref, vp_ref, o_ref, kf_ref, vf_ref,
                *, ts):
    i = pl.program_id(1)
    kv_w = SWA_KV_HEADS * SWA_HD
    kf_ref[0:SWA_BLK] = kp_ref[...]
    kf_ref[SWA_BLK:] = k_ref[...]
    vf_ref[0:SWA_BLK, 0:kv_w] = vp_ref[...]
    vf_ref[SWA_BLK:, 0:kv_w] = v_ref[...]
    vf_ref[:, kv_w:] = jnp.ones((ts + SWA_BLK, LANES), BF16)
    low_half = lax.broadcasted_iota(jnp.int32, (SWA_BLK, LANES), 1) < SWA_HD
    group = SWA_HEADS // SWA_KV_HEADS
    rows = SWA_HEADS * SWA_BLK
    sink = sink_ref[...]
    for n in range(ts // SWA_BLK):
        r0 = n * SWA_BLK
        q = q_ref[:, r0:r0 + SWA_BLK, :].reshape(rows, LANES)
        prev_pen = jnp.where(jnp.logical_and(i == 0, n == 0), NEG_INF, 0.0).astype(F32)
        sp = _dot_nt(q, kf_ref[r0:r0 + SWA_BLK]) + bp_ref[...] + prev_pen
        sc = _dot_nt(q, kf_ref[r0 + SWA_BLK:r0 + 2 * SWA_BLK]) + bc_ref[...]
        m = jnp.maximum(jnp.max(jnp.maximum(sp, sc), axis=-1, keepdims=True), sink)
        ep = jnp.exp(sp - m).astype(BF16)
        ec = jnp.exp(sc - m).astype(BF16)
        o_ext = _dot(ep, vf_ref[r0:r0 + SWA_BLK]) + _dot(ec, vf_ref[r0 + SWA_BLK:r0 + 2 * SWA_BLK])
        o = o_ext[:, :kv_w] / (o_ext[:, kv_w:] + jnp.exp(sink - m))
        for pair in range(SWA_HEADS // 2):
            kv = (2 * pair) // group
            oe = o[(2 * pair) * SWA_BLK:(2 * pair + 1) * SWA_BLK]
            oo = o[(2 * pair + 1) * SWA_BLK:(2 * pair + 2) * SWA_BLK]
            if kv == 0:
                both = jnp.where(low_half, oe, pltpu.roll(oo, SWA_HD, 1))
            else:
                both = jnp.where(low_half, pltpu.roll(oe, SWA_HD, 1), oo)
            o_ref[r0:r0 + SWA_BLK, pair * LANES:(pair + 1) * LANES] = both.astype(BF16)


def _swa_bias():
    r = jnp.arange(SWA_HEADS * SWA_BLK)
    a = (r % SWA_BLK)[:, None]
    slope = 2.0 ** (-(8.0 / SWA_HEADS) * ((r // SWA_BLK) + 1).astype(F32))[:, None]
    j = jnp.arange(SWA_BLK)[None, :]
    dist_cur = (a - j).astype(F32)
    bias_prev = jnp.where(j > a, -slope * (dist_cur + float(SWA_BLK)), NEG_INF)
    bias_cur = jnp.where(j <= a, -slope * dist_cur, NEG_INF)
    return bias_prev.astype(F32), bias_cur.astype(F32)


def _swa(sinks, qs, ks, vs, B, S, ts):
    T = B * S
    nt = S // ts
    kv_w = SWA_KV_HEADS * SWA_HD
    blk_per_tile = ts // SWA_BLK
    rows = SWA_HEADS * SWA_BLK
    bias_prev, bias_cur = _swa_bias()
    sink_rows = jnp.repeat(sinks.astype(F32), SWA_BLK).reshape(rows, 1)
    cur = lambda b, i: (b * nt + i, 0)
    prev = lambda b, i: (jnp.maximum((b * nt + i) * blk_per_tile - 1, 0), 0)
    return pl.pallas_call(
        functools.partial(_swa_kernel, ts=ts),
        grid=(B, nt),
        in_specs=[_const_spec((rows, 1)), _const_spec((rows, SWA_BLK)), _const_spec((rows, SWA_BLK)),
                  pl.BlockSpec((SWA_HEADS, ts, LANES), lambda b, i: (0, b * nt + i, 0)),
                  pl.BlockSpec((ts, kv_w), cur), pl.BlockSpec((SWA_BLK, kv_w), prev),
                  pl.BlockSpec((ts, kv_w), cur), pl.BlockSpec((SWA_BLK, kv_w), prev)],
        out_specs=pl.BlockSpec((ts, SWA_HEADS * SWA_HD), cur),
        out_shape=jax.ShapeDtypeStruct((T, SWA_HEADS * SWA_HD), BF16),
        scratch_shapes=[pltpu.VMEM((ts + SWA_BLK, kv_w), BF16),
                        pltpu.VMEM((ts + SWA_BLK, kv_w + LANES), BF16)],
        compiler_params=_params("arbitrary", "arbitrary"),
        name="swa",
    )(sink_rows, bias_prev, bias_cur, qs, ks, ks, vs, vs)


def _mid_kernel(x_ref, ol_ref, ob_ref, km_ref, vm_ref, wuv_ref, onm_ref, ons_ref, wout_ref,
                ncross_ref, wcq_ref, wco_ref, nffn_ref, wqT_ref, keys_ref,
                x2_ref, hf_ref, scT_ref):
    ol = ol_ref[...]
    o_a = jnp.concatenate(
        [_dot(ol[:, hd * MLA_KV_RANK:(hd + 1) * MLA_KV_RANK], wuv_ref[hd]) for hd in range(MLA_HEADS)],
        axis=1)
    mix = jnp.concatenate([_rms(o_a, onm_ref[...]), _rms(ob_ref[...].astype(F32), ons_ref[...])],
                          axis=1).astype(BF16)
    x1 = x_ref[...] + _dot(mix, wout_ref[...])

    hc = _rms(x1, ncross_ref[...]).astype(BF16)
    q = _dot(hc, wcq_ref[...]) * (float(X_HD) ** -0.5)
    heads = []
    for hd in range(X_HEADS):
        sl = slice(hd * X_HD, (hd + 1) * X_HD)
        s = _dot_nt(q[:, sl].astype(BF16), km_ref[0, :, sl])
        e = jnp.exp(s - jnp.max(s, axis=-1, keepdims=True))
        p = e / jnp.sum(e, axis=-1, keepdims=True)
        heads.append(_dot(p.astype(BF16), vm_ref[0, :, sl]))
    x2 = x1 + _dot(jnp.concatenate(heads, axis=1).astype(BF16), wco_ref[...])
    x2_ref[...] = x2

    hf = _rms(x2, nffn_ref[...]).astype(BF16)
    hf_ref[...] = hf
    qpT = _dot_nt(wqT_ref[...], hf)
    for hc_i in range(PEER_HEADS * 2):
        sl = slice(hc_i * PEER_HALF, (hc_i + 1) * PEER_HALF)
        scT_ref[sl, :] = _dot(keys_ref[hc_i], qpT[sl].astype(BF16))


def _mid(x2d, o_lat, o_b, k_mem, v_mem, w_uv, onm, ons, w_out, ncross, w_cq, w_co, nffn, wqT, keys2,
         S, tile):
    T, D = x2d.shape
    row = lambda i: (i, 0)
    mem_idx = lambda i: ((i * tile) // S, 0, 0)
    n_sc = PEER_HEADS * 2 * N_KEYS
    consts = [w_uv, onm, ons, w_out, ncross, w_cq, w_co, nffn, wqT, keys2]
    return pl.pallas_call(
        _mid_kernel,
        grid=(T // tile,),
        in_specs=[pl.BlockSpec((tile, D), row),
                  pl.BlockSpec((tile, o_lat.shape[1]), row),
                  pl.BlockSpec((tile, o_b.shape[1]), row),
                  pl.BlockSpec((1,) + k_mem.shape[1:], mem_idx),
                  pl.BlockSpec((1,) + v_mem.shape[1:], mem_idx)]
                 + [_const_spec(c.shape) for c in consts],
        out_specs=[pl.BlockSpec((tile, D), row), pl.BlockSpec((tile, D), row),
                   pl.BlockSpec((n_sc, tile), lambda i: (0, i))],
        out_shape=[jax.ShapeDtypeStruct((T, D), F32), jax.ShapeDtypeStruct((T, D), BF16),
                   jax.ShapeDtypeStruct((n_sc, T), F32)],
        compiler_params=_params("arbitrary"),
        name="mid",
    )(x2d, o_lat, o_b, k_mem, v_mem, *consts)


_CAND_ROWS = [(0, 0), (0, 8)] + [(a, 0) for a in range(1, 8)]


def _topk_head_pair(load, tk, between=(None, None), lockstep=True):
    key_iota = lax.broadcasted_iota(jnp.int32, (N_KEYS, tk), 0)
    k_iota = lax.broadcasted_iota(jnp.int32, (PEER_TOPK, tk), 0)
    sub8 = lax.broadcasted_iota(jnp.int32, (8, tk), 0)
    pos = jnp.concatenate([a * PEER_TOPK + b0 + sub8 for a, b0 in _CAND_ROWS]
                          + [(sub8 + 8) * PEER_TOPK], axis=0)

    zero_s = jnp.zeros((PEER_TOPK, tk), F32)
    zero_i = jnp.zeros((PEER_TOPK, tk), jnp.int32)

    def extract(problems, tie_keys, n_tie):
        def body(k, carry):
            sel = k_iota == k
            out = []
            for (vals, s_out, i_out), tkey in zip(carry, tie_keys):
                m = jnp.max(vals, axis=0, keepdims=True)
                c = jnp.where(vals == m, tkey, n_tie)
                first = jnp.min(c, axis=0, keepdims=True)
                out.append((jnp.where(c == first, NEG_INF, vals),
                            jnp.where(sel, m, s_out), jnp.where(sel, first, i_out)))
            return tuple(out)

        init = tuple((v, zero_s, zero_i) for v in problems)
        res = lax.fori_loop(0, PEER_TOPK, body, init, unroll=True)
        return [(s, i) for _, s, i in res]

    def candidates(s0, i0, s1, i1):
        cs = jnp.concatenate([s0[a:a + 1] + s1[b0:b0 + 8] for a, b0 in _CAND_ROWS]
                             + [s0[8:16] + s1[0:1]], axis=0)
        ci = jnp.concatenate([i0[a:a + 1] * N_KEYS + i1[b0:b0 + 8] for a, b0 in _CAND_ROWS]
                             + [i0[8:16] * N_KEYS + i1[0:1]], axis=0)
        return cs, pos * (N_KEYS * N_KEYS) + ci

    def tree(op, xs):
        while len(xs) > 1:
            xs = [op(xs[i], xs[i + 1]) for i in range(0, len(xs) - 1, 2)] + xs[len(xs) & ~1:]
        return xs[0]

    def extract_keys(vals):
        groups = [vals[8 * r:8 * r + 8] for r in range(N_KEYS // 8)]
        s_out, i_out = zero_s, zero_i
        for k in range(PEER_TOPK):
            m = jnp.max(tree(jnp.maximum, groups), axis=0, keepdims=True)
            r_min = tree(jnp.minimum, [jnp.where(g == m, r, N_KEYS // 8) for r, g in enumerate(groups)])
            first = jnp.min(r_min * 8 + sub8, axis=0, keepdims=True)
            d = first - sub8
            groups = [jnp.where(d == 8 * r, NEG_INF, g) for r, g in enumerate(groups)]
            sel = k_iota == k
            s_out = jnp.where(sel, m, s_out)
            i_out = jnp.where(sel, first, i_out)
        return s_out, i_out

    def extract_all(problems, tie_keys, n_tie):
        if lockstep:
            return extract(problems, tie_keys, n_tie)
        return [extract([p], [t], n_tie)[0] for p, t in zip(problems, tie_keys)]

    halves = []
    for h in range(2):
        if lockstep:
            halves.append(extract([load(2 * h), load(2 * h + 1)], [key_iota, key_iota], N_KEYS))
        else:
            halves.append([extract_keys(load(2 * h)), extract_keys(load(2 * h + 1))])
        if between[h] is not None:
            between[h]()
    cands = [candidates(s0, i0, s1, i1) for (s0, i0), (s1, i1) in halves]
    best = extract_all([c[0] for c in cands], [c[1] for c in cands], 1 << 30)
    out = []
    for bs, key in best:
        bi = key & (N_KEYS * N_KEYS - 1)
        e = jnp.exp(bs - jnp.max(bs, axis=0, keepdims=True))
        g = e / jnp.sum(e, axis=0, keepdims=True)
        out.append((g, (bi >> 7).astype(F32), (bi & (N_KEYS - 1)).astype(F32)))
    return out


_PAIR_ROWS = 2 * 2 * N_KEYS
_PAIR_SEL = 2 * PEER_TOPK


def _topk_kernel(sc_ref, i_ref, j_ref, g_ref, is_ref, js_ref, gs_ref):
    tk = sc_ref.shape[1]

    def head_pair(hp, carry):
        base = pl.multiple_of(hp * _PAIR_ROWS, _PAIR_ROWS)
        res = _topk_head_pair(lambda r: sc_ref[pl.ds(base + r * N_KEYS, N_KEYS), :], tk)
        for h, (g, fi, fj) in enumerate(res):
            rows = pl.ds(pl.multiple_of(hp * _PAIR_SEL + h * PEER_TOPK, PEER_TOPK), PEER_TOPK)
            gs_ref[rows, :] = g
            is_ref[rows, :] = fi
            js_ref[rows, :] = fj
        return carry

    lax.fori_loop(0, PEER_HEADS // 2, head_pair, 0)
    i_ref[...] = is_ref[...].T
    j_ref[...] = js_ref[...].T
    g_ref[...] = gs_ref[...].T


def _topk_first_tile(scT, tile):
    n_sc = scT.shape[0]
    n_sel = PEER_HEADS * PEER_TOPK
    out = jax.ShapeDtypeStruct((tile, n_sel), F32)
    return pl.pallas_call(
        _topk_kernel,
        grid=(tile // LANES,),
        in_specs=[pl.BlockSpec((n_sc, LANES), lambda i: (0, i))],
        out_specs=[pl.BlockSpec((LANES, n_sel), lambda i: (i, 0))] * 3,
        out_shape=[out] * 3,
        scratch_shapes=[pltpu.VMEM((n_sel, LANES), F32)] * 3,
        compiler_params=_params("arbitrary"),
        name="topk",
    )(scT)


_W_PAD = 8
_I_SPLIT = 2


def _peer_kernel(hf_ref, i0_ref, j0_ref, g0_ref, sc_ref, uT_ref, v_ref, x2_ref, nf_ref, o_ref,
                 w_ref, stash_ref, y_ref, i_ref, j_ref, g_ref, in_ref, jn_ref, gn_ref, *, tile, n_blk):
    t_idx = pl.program_id(0)
    half = pl.program_id(1)
    s = pl.program_id(2)
    pitch = tile + _W_PAD
    n_i = N_KEYS // _I_SPLIT
    first_step = jnp.logical_and(half == 0, s == 0)

    @pl.when(jnp.logical_and(first_step, t_idx == 0))
    def _():
        i_ref[...] = i0_ref[...]
        j_ref[...] = j0_ref[...]
        g_ref[...] = g0_ref[...]

    @pl.when(jnp.logical_and(first_step, t_idx > 0))
    def _():
        for lb in range(tile // LANES):
            rows = slice(lb * LANES, (lb + 1) * LANES)
            i_ref[rows, :] = in_ref[lb].T
            j_ref[rows, :] = jn_ref[lb].T
            g_ref[rows, :] = gn_ref[lb].T

    @pl.when(first_step)
    def _():
        sub = lax.broadcasted_iota(jnp.int32, (N_KEYS, N_KEYS), 0).astype(F32)

        def tok(t, carry):
            row = pl.ds(t, 1)
            a_t = jnp.where(sub == i_ref[row, :], 0.5 * g_ref[row, :], 0.0).astype(BF16)
            b_t = jnp.where(sub == j_ref[row, :], 1.0, 0.0).astype(BF16)
            w = _dot_nt(a_t, b_t)
            w_ref[pl.ds(t, n_i, stride=pitch), :] = w[:n_i]
            stash_ref[pl.ds(pl.multiple_of(t * n_i, n_i), n_i), :] = w[n_i:].astype(BF16)
            return carry

        lax.fori_loop(0, tile, tok, 0, unroll=64)
        y_ref[...] = jnp.zeros(y_ref.shape, F32)

    @pl.when(jnp.logical_and(half == 1, s == 0))
    def _():
        def tok(t, carry):
            rows = pl.ds(pl.multiple_of(t * n_i, n_i), n_i)
            w_ref[pl.ds(t, n_i, stride=pitch), :] = stash_ref[rows, :].astype(F32)
            return carry

        lax.fori_loop(0, tile, tok, 0, unroll=64)

    hf = hf_ref[...]
    gated = []

    def expert_blocks(c0, c1):
        for c in range(c0, c1, 2):
            act = _dot(hf, uT_ref[:, c * N_KEYS:(c + 2) * N_KEYS])
            for d in range(2):
                a = act[:, d * N_KEYS:(d + 1) * N_KEYS]
                w = w_ref[pl.ds(pl.multiple_of((s * n_blk + c + d) * pitch, 8), tile), :]
                gated.append((a * (1.0 + lax.erf(a * (0.5 ** 0.5))) * w).astype(BF16))

    u = half * pl.num_programs(2) + s
    pairs = PEER_HEADS // 2
    lb = u // pairs
    res = _topk_head_pair(lambda r: sc_ref[r * N_KEYS:(r + 1) * N_KEYS, :], LANES,
                          between=(lambda: expert_blocks(0, n_blk // 2),
                                   lambda: expert_blocks(n_blk // 2, n_blk)), lockstep=False)
    y_ref[...] += _dot(jnp.concatenate(gated, axis=1), v_ref[...])

    for h, (g, fi, fj) in enumerate(res):
        rows = pl.ds(pl.multiple_of((u % pairs) * _PAIR_SEL + h * PEER_TOPK, PEER_TOPK), PEER_TOPK)
        gn_ref[lb, rows, :] = g
        in_ref[lb, rows, :] = fi
        jn_ref[lb, rows, :] = fj

    @pl.when(jnp.logical_and(half == _I_SPLIT - 1, s == pl.num_programs(2) - 1))
    def _():
        o_ref[...] = _rms(x2_ref[...] + y_ref[...], nf_ref[...])


def _peer(hf, sel0_i, sel0_j, sel0_g, scT, uT, v, x2, norm_final, tile, n_blk):
    T, D = x2.shape
    n_exp = v.shape[0]
    n_sel = sel0_i.shape[1]
    ew = n_blk * N_KEYS
    steps = n_exp // ew // _I_SPLIT
    assert steps * n_blk * _I_SPLIT == N_KEYS
    n_tiles = T // tile
    blocks = tile // LANES
    pairs = PEER_HEADS // 2
    assert _I_SPLIT * steps == pairs * blocks, "one retrieval unit of the next tile per grid step"
    row = lambda t, h, s: (t, 0)

    def next_scores(t, h, s):
        u = h * steps + s
        return (u % pairs, jnp.minimum(t + 1, n_tiles - 1) * blocks + u // pairs)

    params = pltpu.CompilerParams(dimension_semantics=("arbitrary",) * 3,
                                  vmem_limit_bytes=PEER_VMEM_LIMIT)
    sel_scratch = [pltpu.VMEM((tile, n_sel), F32)] * 3 + [pltpu.VMEM((blocks, n_sel, LANES), F32)] * 3
    return pl.pallas_call(
        functools.partial(_peer_kernel, tile=tile, n_blk=n_blk),
        grid=(n_tiles, _I_SPLIT, steps),
        in_specs=[pl.BlockSpec((tile, D), row),
                  _const_spec((tile, n_sel)), _const_spec((tile, n_sel)), _const_spec((tile, n_sel)),
                  pl.BlockSpec((_PAIR_ROWS, LANES), next_scores),
                  pl.BlockSpec((D, ew), lambda t, h, s: (0, h * steps + s)),
                  pl.BlockSpec((ew, D), lambda t, h, s: (h * steps + s, 0)),
                  pl.BlockSpec((tile, D), row),
                  _const_spec(norm_final.shape)],
        out_specs=pl.BlockSpec((tile, D), row),
        out_shape=jax.ShapeDtypeStruct((T, D), F32),
        scratch_shapes=[pltpu.VMEM((N_KEYS // _I_SPLIT * (tile + _W_PAD), N_KEYS), F32),
                        pltpu.VMEM((tile * N_KEYS // _I_SPLIT, N_KEYS), BF16),
                        pltpu.VMEM((tile, D), F32)] + sel_scratch,
        compiler_params=params,
        name="peer",
    )(hf, sel0_i, sel0_j, sel0_g, scT, uT, v, x2, norm_final)


def _tile(n, want):
    t = min(n, want)
    assert n % t == 0, (n, t)
    return t


def _tiles(S):
    kb = _tile(S, 512)
    return dict(proj=kb, mla_keys=kb, mla_queries=_tile(kb, 256), swa=_tile(S, 512),
                mid=_tile(S, 512), peer=_tile(S, 512),
                peer_blocks=8)


def _layer(x2d, mem, pos2d, B, S, norm_mix, w_in, q_a_norm, w_q_b, kv_a_norm, w_kv_b, swa_sinks,
           out_norm_mla, out_norm_swa, w_out, norm_cross, norm_mem, w_cq, w_ck, w_cv, w_co,
           norm_ffn, peer_w_q, peer_keys, peer_u, peer_v):
    D = x2d.shape[1]
    row = lambda g: g.reshape(1, -1)

    o = MLA_Q_RANK + MLA_KV_RANK
    w_kr = w_in[:, o:o + MLA_ROPE]
    o += MLA_ROPE
    w_qs = w_in[:, o:o + SWA_HEADS * SWA_HD].reshape(D, SWA_HEADS, SWA_HD)
    o += SWA_HEADS * SWA_HD
    w_ks = w_in[:, o:o + SWA_KV_HEADS * SWA_HD]
    o += SWA_KV_HEADS * SWA_HD
    w_vs = w_in[:, o:o + SWA_KV_HEADS * SWA_HD]
    group = SWA_HEADS // SWA_KV_HEADS
    zeros = jnp.zeros((D, SWA_HD), w_in.dtype)
    qs_slots = [jnp.concatenate([w_qs[:, hh], zeros] if hh // group == 0 else [zeros, w_qs[:, hh]], axis=1)
                for hh in range(SWA_HEADS)]
    half = MLA_ROPE // 2
    swap = lambda w: jnp.concatenate([w[:, half:], w[:, :half]], axis=1)
    w_in_r = jnp.concatenate([w_in[:, :MLA_Q_RANK + MLA_KV_RANK]] + qs_slots
                             + [w_ks, w_vs, w_kr, swap(w_kr)], axis=1).astype(BF16)

    wq = w_q_b.reshape(MLA_Q_RANK, MLA_HEADS, MLA_NOPE + MLA_ROPE)
    q_nope = wq[:, :, :MLA_NOPE].reshape(MLA_Q_RANK, MLA_HEADS * MLA_NOPE)
    q_rope = [jnp.concatenate([wq[:, hd, MLA_NOPE:], swap(wq[:, hd, MLA_NOPE:])], axis=1)
              for hd in range(MLA_HEADS)]
    w_qb_r = jnp.concatenate([q_nope] + q_rope, axis=1).astype(BF16)

    wkv = w_kv_b.reshape(MLA_KV_RANK, MLA_HEADS, MLA_NOPE + MLA_V)
    w_ukT = jnp.transpose(wkv[:, :, :MLA_NOPE], (1, 2, 0)).astype(BF16)
    w_uv = jnp.transpose(wkv[:, :, MLA_NOPE:], (1, 0, 2)).astype(BF16)

    inv = ROPE_THETA ** (-jnp.arange(half, dtype=F32) / half)
    inv_slot = jnp.tile(inv, 2 * MLA_ROPE // half).reshape(1, 2 * MLA_ROPE)
    sgn_slot = jnp.concatenate([jnp.ones((MLA_ROPE,), F32), -jnp.ones((half,), F32),
                                jnp.ones((half,), F32)]).reshape(1, 2 * MLA_ROPE)

    w_ckv = jnp.concatenate([w_ck, w_cv], axis=1).astype(BF16)
    wqT = peer_w_q.T.astype(BF16)
    keys2 = peer_keys.reshape(PEER_HEADS * 2, N_KEYS, PEER_HALF).astype(BF16)
    uT = peer_u.T.astype(BF16)
    v_b = peer_v.astype(BF16)

    tiles = _tiles(S)
    k_mem, v_mem = _mem_kv(mem, row(norm_mem), w_ckv)
    qcat, kcat, ct_blk, qs, ks, vs = _proj(x2d, pos2d, inv_slot, sgn_slot, row(norm_mix), w_in_r,
                                           row(q_a_norm), w_qb_r, row(kv_a_norm), w_ukT, tiles["proj"])
    o_lat = _mla(qcat, kcat.reshape(-1, tiles["mla_keys"], MLA_QK), ct_blk, B, tiles["mla_queries"])
    o_b = _swa(swa_sinks, qs, ks, vs, B, S, tiles["swa"])
    x2, hf, scT = _mid(x2d, o_lat, o_b, k_mem, v_mem, w_uv, row(out_norm_mla), row(out_norm_swa),
                       w_out.astype(BF16), row(norm_cross), w_cq.astype(BF16), w_co.astype(BF16),
                       row(norm_ffn), wqT, keys2, S, tiles["mid"])
    return hf, scT, uT, v_b, x2


def kernel(x, mem, positions, norm_mix, w_in, q_a_norm, w_q_b, kv_a_norm, w_kv_b, swa_sinks,
           out_norm_mla, out_norm_swa, w_out, norm_cross, norm_mem, w_cq, w_ck, w_cv, w_co,
           norm_ffn, peer_w_q, peer_keys, peer_u, peer_v, norm_final):
    B, S, D = x.shape
    depth = norm_mix.shape[0]
    assert depth == 1, "the final rmsnorm is fused into the last layer's PEER kernel"
    x2d = x.reshape(B * S, D)
    pos2d = positions.reshape(B * S, 1)
    l = 0
    hf, scT, uT, v_b, x2 = _layer(
        x2d, mem, pos2d, B, S, norm_mix[l], w_in[l], q_a_norm[l], w_q_b[l], kv_a_norm[l], w_kv_b[l],
        swa_sinks[l], out_norm_mla[l], out_norm_swa[l], w_out[l], norm_cross[l], norm_mem[l],
        w_cq[l], w_ck[l], w_cv[l], w_co[l], norm_ffn[l], peer_w_q[l], peer_keys[l], peer_u[l],
        peer_v[l])
    tiles = _tiles(S)
    sel0 = _topk_first_tile(scT, tiles["peer"])
    out = _peer(hf, *sel0, scT, uT, v_b, x2, norm_final.reshape(1, D), tiles["peer"],
                tiles["peer_blocks"])
    return out.reshape(B, S, D)
```
